```python
import math
import jax, jax.numpy as jnp
from jax import lax
import numpy as np

D_MODEL = 2048
BATCH = 2
SEQ = 16384
DEPTH = 4

SSD_EXPAND = 2
SSD_D_INNER = SSD_EXPAND * D_MODEL
SSD_HEAD_DIM = 64
SSD_N_HEADS = SSD_D_INNER // SSD_HEAD_DIM
SSD_N_GROUPS = 8
SSD_HEADS_PER_GROUP = SSD_N_HEADS // SSD_N_GROUPS
SSD_D_STATE = 128
SSD_CONV_WIDTH = 5
SSD_CHUNK = 128
SSD_CONV_DIM = SSD_D_INNER + 2 * SSD_N_GROUPS * SSD_D_STATE

MLA_N_HEADS = 16
MLA_Q_RANK = 512
MLA_KV_RANK = 512
MLA_NOPE_DIM = 128
MLA_ROPE_DIM = 64
MLA_V_DIM = 128
MLA_QK_DIM = MLA_NOPE_DIM + MLA_ROPE_DIM
ROPE_THETA = 10000.0
ATTN_BLOCK = 128

FFN_HIDDEN = -(-8 * D_MODEL // (3 * 256)) * 256

DEEPNORM_ALPHA = (2 * DEPTH) ** 0.25
DEEPNORM_BETA = (8 * DEPTH) ** -0.25

IN_SIZES = (SSD_D_INNER, SSD_CONV_DIM, 2 * SSD_N_HEADS, MLA_Q_RANK, MLA_KV_RANK + MLA_ROPE_DIM, D_MODEL, D_MODEL)
IN_DIM = sum(IN_SIZES)

kernel_name = 'hybrid_ssd_mla_deepnorm_encoder'


def split_last(t, sizes):
    offsets = np.cumsum(np.array(sizes))[:-1].tolist()
    return jnp.split(t, offsets, axis=-1)


def rms_norm(x, w, eps=1e-6):
    xf = x.astype(jnp.float32)
    y = xf * lax.rsqrt(jnp.mean(xf * xf, axis=-1, keepdims=True) + eps)
    return (y * w.astype(jnp.float32)).astype(x.dtype)


def layer_norm(x, g, b, eps=1e-5):
    xf = x.astype(jnp.float32)
    mu = jnp.mean(xf, axis=-1, keepdims=True)
    var = jnp.mean(jnp.square(xf - mu), axis=-1, keepdims=True)
    y = (xf - mu) * lax.rsqrt(var + eps)
    return (y * g.astype(jnp.float32) + b.astype(jnp.float32)).astype(x.dtype)


def rope_tables(seq_len):
    pos = jnp.arange(seq_len, dtype=jnp.float32)
    inv_freq = ROPE_THETA ** (-jnp.arange(0, MLA_ROPE_DIM, 2, dtype=jnp.float32) / MLA_ROPE_DIM)
    ang = pos[:, None] * inv_freq[None, :]
    return jnp.cos(ang), jnp.sin(ang)


def apply_rope(x, cos, sin):
    half = x.shape[-1] // 2
    x1, x2 = x[..., :half], x[..., half:]
    cos = cos.astype(x.dtype)
    sin = sin.astype(x.dtype)
    return jnp.concatenate([x1 * cos - x2 * sin, x2 * cos + x1 * sin], axis=-1)


def centred_depthwise_conv(x, w, b):
    width, ch = w.shape
    y = lax.conv_general_dilated(
        x, w[:, None, :].astype(x.dtype), window_strides=(1,),
        padding=[(width // 2, width // 2)],
        dimension_numbers=('NWC', 'WIO', 'NWC'), feature_group_count=ch)
    return y + b.astype(x.dtype)


def ssd_chunk_scan(xh, a, b_in, c_in):
    bsz, seq = xh.shape[:2]
    n_chunks = seq // SSD_CHUNK
    mask = jnp.tril(jnp.ones((SSD_CHUNK, SSD_CHUNK), dtype=bool))[None, :, :, None, None]

    def to_chunks(t):
        return jnp.moveaxis(t.reshape(bsz, n_chunks, SSD_CHUNK, *t.shape[2:]), 1, 0)

    def step(state, inp):
        xc, ac, bc, cc = inp
        acum = jnp.cumsum(ac, axis=1)
        seg = acum[:, :, None] - acum[:, None, :]
        decay = jnp.exp(jnp.where(mask, seg, -jnp.inf))
        cb = jnp.einsum('bign,bjgn->bijg', cc, bc)
        y_intra = jnp.einsum('bijg,bijgh,bjghp->bighp', cb, decay, xc)
        y_inter = jnp.einsum('bign,bghpn,bigh->bighp', cc, state, jnp.exp(acum))
        a_tot = acum[:, -1]
        w_state = jnp.exp(a_tot[:, None] - acum)
        new_state = state * jnp.exp(a_tot)[..., None, None] + jnp.einsum('bjgh,bjghp,bjgn->bghpn', w_state, xc, bc)
        return new_state, y_intra + y_inter

    init = jnp.zeros((bsz, SSD_N_GROUPS, SSD_HEADS_PER_GROUP, SSD_HEAD_DIM, SSD_D_STATE), jnp.float32)
    _, ys = lax.scan(step, init, (to_chunks(xh), to_chunks(a), to_chunks(b_in), to_chunks(c_in)))
    return jnp.moveaxis(ys, 0, 1).reshape(bsz, seq, SSD_N_GROUPS, SSD_HEADS_PER_GROUP, SSD_HEAD_DIM)


def ssd_branch(z, xbc, dt_raw, conv_w, conv_b, a_log, dt_bias, d_skip, norm_w, w_o):
    bsz, seq = z.shape[:2]
    g, hg = SSD_N_GROUPS, SSD_HEADS_PER_GROUP
    xbc = jax.nn.silu(centred_depthwise_conv(xbc, conv_w, conv_b)).astype(jnp.float32)
    xs, b_in, c_in = jnp.split(xbc, [SSD_D_INNER, SSD_D_INNER + g * SSD_D_STATE], axis=-1)
    xs = xs.reshape(bsz, seq, g, hg, SSD_HEAD_DIM)
    b_in = b_in.reshape(bsz, seq, g, SSD_D_STATE)
    c_in = c_in.reshape(bsz, seq, g, SSD_D_STATE)
    dt = jax.nn.softplus(dt_raw.astype(jnp.float32).reshape(bsz, seq, 2, SSD_N_HEADS) + dt_bias.astype(jnp.float32))
    dt = dt.reshape(bsz, seq, 2, g, hg)
    a = -jnp.exp(a_log.astype(jnp.float32)).reshape(2, g, hg)
    y = d_skip.astype(jnp.float32).reshape(g, hg, 1) * xs
    dt_f = dt[:, :, 0]
    y = y + ssd_chunk_scan(xs * dt_f[..., None], dt_f * a[0], b_in, c_in)
    dt_b = dt[:, :, 1]
    flip = lambda t: jnp.flip(t, axis=1)
    y = y + flip(ssd_chunk_scan(flip(xs * dt_b[..., None]), flip(dt_b * a[1]), flip(b_in), flip(c_in)))
    y = y.reshape(bsz, seq, SSD_D_INNER) * jax.nn.silu(z.astype(jnp.float32))
    y = rms_norm(y.reshape(bsz, seq, g, SSD_D_INNER // g), norm_w.reshape(g, SSD_D_INNER // g))
    return y.reshape(bsz, seq, SSD_D_INNER).astype(z.dtype) @ w_o


def block_attention(q_nope, q_rope, k_nope, k_rope, v):
    bsz, seq, heads, _ = q_nope.shape
    n_blocks = seq // ATTN_BLOCK
    scale = MLA_QK_DIM ** -0.5

    def to_blocks(t):
        return jnp.moveaxis(t.reshape(bsz, n_blocks, ATTN_BLOCK, *t.shape[2:]), 1, 0)

    def one_block(args):
        qn_b, qr_b = args
        s = jnp.einsum('bqhd,bkhd->bhqk', qn_b, k_nope) + jnp.einsum('bqhr,bkr->bhqk', qr_b, k_rope)
        p = jax.nn.softmax(s.astype(jnp.float32) * scale, axis=-1).astype(v.dtype)
        return jnp.einsum('bhqk,bkhd->bqhd', p, v)

    o = lax.map(one_block, (to_blocks(q_nope), to_blocks(q_rope)))
    return jnp.moveaxis(o, 0, 1).reshape(bsz, seq, heads * v.shape[-1])


def mla_branch(q_lat, kv_lat, q_norm_w, w_uq, kv_norm_w, w_ukv, w_o, cos, sin):
    bsz, seq = q_lat.shape[:2]
    q = (rms_norm(q_lat, q_norm_w) @ w_uq).reshape(bsz, seq, MLA_N_HEADS, MLA_QK_DIM)
    q_nope = q[..., :MLA_NOPE_DIM]
    q_rope = apply_rope(q[..., MLA_NOPE_DIM:], cos[:, None, :], sin[:, None, :])
    c_kv = kv_lat[..., :MLA_KV_RANK]
    k_rope = apply_rope(kv_lat[..., MLA_KV_RANK:], cos, sin)
    kv = (rms_norm(c_kv, kv_norm_w) @ w_ukv).reshape(bsz, seq, MLA_N_HEADS, MLA_NOPE_DIM + MLA_V_DIM)
    k_nope, v = kv[..., :MLA_NOPE_DIM], kv[..., MLA_NOPE_DIM:]
    return block_attention(q_nope, q_rope, k_nope, k_rope, v) @ w_o


def setup_inputs(seed: int = 0) -> dict:
    key = jax.random.key(seed)
    ks = jax.random.split(key, 22)
    f32 = jnp.float32

    def normal(k, shape, scale):
        return jax.random.normal(k, shape, f32) * scale

    def gain(k, shape):
        return 1.0 + 0.1 * jax.random.normal(k, shape, f32)

    dt_init = jnp.exp(jax.random.uniform(ks[5], (DEPTH, 2, SSD_N_HEADS), f32, math.log(1e-3), math.log(1e-1)))
    return {
        'x': jax.random.normal(ks[0], (BATCH, SEQ, D_MODEL), f32),
        'w_in': normal(ks[1], (DEPTH, D_MODEL, IN_DIM), D_MODEL ** -0.5),
        'conv_w': normal(ks[2], (DEPTH, SSD_CONV_WIDTH, SSD_CONV_DIM), SSD_CONV_WIDTH ** -0.5),
        'conv_b': normal(ks[3], (DEPTH, SSD_CONV_DIM), 0.01),
        'ssd_a_log': jnp.log(jax.random.uniform(ks[4], (DEPTH, 2, SSD_N_HEADS), f32, 1.0, 16.0)),
        'ssd_dt_bias': dt_init + jnp.log(-jnp.expm1(-dt_init)),
        'ssd_d': gain(ks[6], (DEPTH, SSD_N_HEADS)),
        'ssd_norm_w': gain(ks[7], (DEPTH, SSD_D_INNER)),
        'w_ssd_out': normal(ks[8], (DEPTH, SSD_D_INNER, D_MODEL), SSD_D_INNER ** -0.5 * DEEPNORM_BETA),
        'mla_q_norm_w': gain(ks[9], (DEPTH, MLA_Q_RANK)),
        'w_uq': normal(ks[10], (DEPTH, MLA_Q_RANK, MLA_N_HEADS * MLA_QK_DIM), MLA_Q_RANK ** -0.5),
        'mla_kv_norm_w': gain(ks[11], (DEPTH, MLA_KV_RANK)),
        'w_ukv': normal(ks[12], (DEPTH, MLA_KV_RANK, MLA_N_HEADS * (MLA_NOPE_DIM + MLA_V_DIM)), MLA_KV_RANK ** -0.5),
        'w_mla_out': normal(ks[13], (DEPTH, MLA_N_HEADS * MLA_V_DIM, D_MODEL), (MLA_N_HEADS * MLA_V_DIM) ** -0.5 * DEEPNORM_BETA),
        'w_out': normal(ks[14], (DEPTH, D_MODEL, D_MODEL), D_MODEL ** -0.5 * DEEPNORM_BETA),
        'ln1_g': gain(ks[15], (DEPTH, D_MODEL)),
        'ln1_b': normal(ks[16], (DEPTH, D_MODEL), 0.01),
        'w_ffn_gate': normal(ks[17], (DEPTH, D_MODEL, FFN_HIDDEN), D_MODEL ** -0.5),
        'w_ffn_up': normal(ks[18], (DEPTH, D_MODEL, FFN_HIDDEN), D_MODEL ** -0.5),
        'w_ffn_down': normal(ks[19], (DEPTH, FFN_HIDDEN, D_MODEL), FFN_HIDDEN ** -0.5 * DEEPNORM_BETA),
        'ln2_g': gain(ks[20], (DEPTH, D_MODEL)),
        'ln2_b': normal(ks[21], (DEPTH, D_MODEL), 0.01),
    }


def reference(x, w_in, conv_w, conv_b, ssd_a_log, ssd_dt_bias, ssd_d, ssd_norm_w, w_ssd_out,
              mla_q_norm_w, w_uq, mla_kv_norm_w, w_ukv, w_mla_out, w_out, ln1_g, ln1_b,
              w_ffn_gate, w_ffn_up, w_ffn_down, ln2_g, ln2_b):
    cos, sin = rope_tables(x.shape[1])
    h = x
    for l in range(DEPTH):
        proj = h @ w_in[l]
        z, xbc, dt_raw, q_lat, kv_lat, g_ssd, g_mla = split_last(proj, IN_SIZES)
        y_ssd = ssd_branch(z, xbc, dt_raw, conv_w[l], conv_b[l], ssd_a_log[l], ssd_dt_bias[l],
                           ssd_d[l], ssd_norm_w[l], w_ssd_out[l])
        y_mla = mla_branch(q_lat, kv_lat, mla_q_norm_w[l], w_uq[l], mla_kv_norm_w[l], w_ukv[l],
                           w_mla_out[l], cos, sin)
        merged = jax.nn.sigmoid(g_ssd) * y_ssd + jax.nn.sigmoid(g_mla) * y_mla
        h = layer_norm(DEEPNORM_ALPHA * h + merged @ w_out[l], ln1_g[l], ln1_b[l])
        ffn = (jax.nn.silu(h @ w_ffn_gate[l]) * (h @ w_ffn_up[l])) @ w_ffn_down[l]
        h = layer_norm(DEEPNORM_ALPHA * h + ffn, ln2_g[l], ln2_b[l])
    return h
```

```python
import functools

import jax
import jax.numpy as jnp
from jax import lax
from jax.experimental import pallas as pl
from jax.experimental.pallas import tpu as pltpu

F32 = jnp.float32
BF16 = jnp.bfloat16

D_MODEL = 2048
DEPTH = 4
SSD_D_INNER = 4096
SSD_HEAD_DIM = 64
SSD_N_HEADS = 64
SSD_N_GROUPS = 8
SSD_HEADS_PER_GROUP = 8
SSD_D_STATE = 128
SSD_CONV_WIDTH = 5
SSD_CHUNK = 128
SSD_GROUP_WIDTH = SSD_HEADS_PER_GROUP * SSD_HEAD_DIM
SSD_BC_WIDTH = SSD_N_GROUPS * SSD_D_STATE
SSD_CONV_DIM = SSD_D_INNER + 2 * SSD_BC_WIDTH
MLA_N_HEADS = 16
MLA_Q_RANK = 512
MLA_KV_RANK = 512
MLA_NOPE_DIM = 128
MLA_ROPE_DIM = 64
MLA_V_DIM = 128
MLA_QK_DIM = MLA_NOPE_DIM + MLA_ROPE_DIM
MLA_QK_PAD = 256
ROPE_THETA = 10000.0
FFN_HIDDEN = 5632
DEEPNORM_ALPHA = (2 * DEPTH) ** 0.25
RMS_EPS = 1e-6
LN_EPS = 1e-5

MAIN_Z = 0
MAIN_XBC = SSD_D_INNER
MAIN_GSSD = MAIN_XBC + SSD_CONV_DIM
MAIN_GMLA = MAIN_GSSD + D_MODEL
MAIN_WIDTH = MAIN_GMLA + D_MODEL
SMALL_Q = 0
SMALL_CKV = MLA_Q_RANK
SMALL_DT = SMALL_CKV + MLA_KV_RANK
SMALL_KROPE = SMALL_DT + 2 * SSD_N_HEADS
SMALL_WIDTH = SMALL_KROPE + 128
ACT_B = SSD_D_INNER
ACT_C = SSD_D_INNER + SSD_BC_WIDTH

V7X_VMEM_BYTES = 64 * 1024 * 1024
VMEM_LIMIT_CAP = 56 * 1024 * 1024
VMEM_LIMIT_FLOOR = 32 * 1024 * 1024
LANE = 128
BF16_SUBLANE_TILE = 16


def _nbytes(shape, dtype):
    n = 1
    for s in shape:
        n *= s
    return n * jnp.dtype(dtype).itemsize


def _params(semantics, blocks, scratch_bytes=0, temp_bytes=0):
    need = 2 * sum(_nbytes(s, d) for s, d in blocks) + scratch_bytes + temp_bytes
    limit = min(max(need + need // 4, VMEM_LIMIT_FLOOR), VMEM_LIMIT_CAP)
    return pltpu.CompilerParams(dimension_semantics=semantics, vmem_limit_bytes=limit)


def _tile(n, pref):
    t = min(n, pref)
    assert n % t == 0, (n, pref)
    return t


def _dot(a, b):
    return jnp.dot(a, b, preferred_element_type=F32)


def _dot_nt(a, b):
    return lax.dot_general(a, b, (((1,), (1,)), ((), ())), preferred_element_type=F32)


def _mm_kernel(a_ref, w_ref, o_ref):
    o_ref[...] = _dot(a_ref[...], w_ref[...]).astype(o_ref.dtype)


def _matmul(a, w, out_dtype, tn, name):
    m, k = a.shape
    n = w.shape[1]
    tm = _tile(m, 1024)
    tn = _tile(n, tn)
    blocks = [((tm, k), a.dtype), ((k, tn), w.dtype), ((tm, tn), out_dtype)]
    return pl.pallas_call(
        _mm_kernel,
        grid=(n // tn, m // tm),
        in_specs=[pl.BlockSpec((tm, k), lambda j, i: (i, 0)), pl.BlockSpec((k, tn), lambda j, i: (0, j))],
        out_specs=pl.BlockSpec((tm, tn), lambda j, i: (i, j)),
        out_shape=jax.ShapeDtypeStruct((m, n), out_dtype),
        compiler_params=_params(("arbitrary", "arbitrary"), blocks, temp_bytes=_nbytes((tm, tn), F32)),
        name=name,
    )(a, w)


def _conv_kernel(prev_ref, cur_ref, next_ref, w_ref, o_ref, *, n_seq_blocks):
    j = pl.program_id(1)
    cur = cur_ref[...].astype(F32)
    ts = cur.shape[0]
    has_prev = j > 0
    has_next = j < n_seq_blocks - 1
    halo = BF16_SUBLANE_TILE
    p0 = jnp.where(has_prev, prev_ref[halo - 2:halo - 1, :].astype(F32), 0.0)
    p1 = jnp.where(has_prev, prev_ref[halo - 1:halo, :].astype(F32), 0.0)
    n0 = jnp.where(has_next, next_ref[0:1, :].astype(F32), 0.0)
    n1 = jnp.where(has_next, next_ref[1:2, :].astype(F32), 0.0)
    row = lax.broadcasted_iota(jnp.int32, cur.shape, 0)
    xm1 = jnp.where(row == 0, p1, pltpu.roll(cur, 1, 0))
    xm2 = jnp.where(row == 0, p0, jnp.where(row == 1, p1, pltpu.roll(cur, 2, 0)))
    xp1 = jnp.where(row == ts - 1, n0, pltpu.roll(cur, ts - 1, 0))
    xp2 = jnp.where(row == ts - 2, n0, jnp.where(row == ts - 1, n1, pltpu.roll(cur, ts - 2, 0)))
    w = w_ref[...]
    y = xm2 * w[0:1] + xm1 * w[1:2] + cur * w[2:3] + xp1 * w[3:4] + xp2 * w[4:5] + w[5:6]
    o_ref[...] = (y * jax.nn.sigmoid(y)).astype(o_ref.dtype)


def _conv_silu(main, conv_prm, batch, seq):
    t = main.shape[0]
    ts = _tile(seq, 512)
    tc = 1024
    nsb = seq // ts
    col0 = MAIN_XBC // tc
    halo = BF16_SUBLANE_TILE
    rpb = ts // halo
    last_halo = t // halo - 1
    blocks = [((halo, tc), BF16), ((ts, tc), BF16), ((halo, tc), BF16), ((8, tc), F32), ((ts, tc), BF16)]
    return pl.pallas_call(
        functools.partial(_conv_kernel, n_seq_blocks=nsb),
        grid=(batch, nsb, SSD_CONV_DIM // tc),
        in_specs=[
            pl.BlockSpec((halo, tc), lambda b, j, c: (jnp.maximum((b * nsb + j) * rpb - 1, 0), col0 + c)),
            pl.BlockSpec((ts, tc), lambda b, j, c: (b * nsb + j, col0 + c)),
            pl.BlockSpec((halo, tc), lambda b, j, c: (jnp.minimum((b * nsb + j + 1) * rpb, last_halo), col0 + c)),
            pl.BlockSpec((8, tc), lambda b, j, c: (0, c)),
        ],
        out_specs=pl.BlockSpec((ts, tc), lambda b, j, c: (b * nsb + j, c)),
        out_shape=jax.ShapeDtypeStruct((t, SSD_CONV_DIM), BF16),
        compiler_params=_params(("arbitrary",) * 3, blocks, temp_bytes=8 * _nbytes((ts, tc), F32)),
        name="conv_silu",
    )(main, main, main, conv_prm)


def _ssd_kernel(xf_ref, bf_ref, cf_ref, dtf_ref, xb_ref, bb_ref, cb_ref, dtb_ref, prm_ref,
                yf_ref, yb_ref, st_ref, col_ref, row_ref, *, groups_per_step):
    q = SSD_CHUNK
    k = pl.program_id(1)
    gi = pl.program_id(2)
    half = SSD_N_HEADS

    @pl.when(jnp.logical_and(k == 0, gi == 0))
    def _zero_state():
        st_ref[...] = jnp.zeros(st_ref.shape, st_ref.dtype)

    @pl.when(gi == 0)
    def _per_chunk_decay_terms():
        lane = lax.broadcasted_iota(jnp.int32, (q, 2 * half), 1)
        is_fwd = lane < half
        x = jnp.where(is_fwd, dtf_ref[...], dtb_ref[...]) + prm_ref[0:1, :]
        dt = jnp.maximum(x, 0.0) + jnp.log1p(jnp.exp(-jnp.abs(x)))
        a = -jnp.exp(prm_ref[1:2, :]) * dt
        r = lax.broadcasted_iota(jnp.int32, (q, q), 0)
        c = lax.broadcasted_iota(jnp.int32, (q, q), 1)
        tril = (r >= c).astype(BF16)
        triu = (r <= c).astype(BF16)
        a_hi = a.astype(BF16)
        a_lo = (a - a_hi.astype(F32)).astype(BF16)
        p = jnp.where(is_fwd, _dot(tril, a_hi) + _dot(tril, a_lo), _dot(triu, a_hi) + _dot(triu, a_lo))
        p_tot = jnp.where(is_fwd[0:1], p[q - 1:q, :], p[0:1, :])
        col_ref[0] = p
        col_ref[1] = jnp.exp(p)
        row_ref[0] = p.T
        row_ref[1] = dt.T
        row_ref[2] = (dt * jnp.exp(p_tot - p)).T
        row_ref[3] = jnp.broadcast_to(jnp.exp(p_tot), (q, 2 * half)).T

    ri = lax.broadcasted_iota(jnp.int32, (q, q), 0)
    ci = lax.broadcasted_iota(jnp.int32, (q, q), 1)
    masks = (ri >= ci, ri <= ci)
    dirs = ((xf_ref, bf_ref, cf_ref, yf_ref), (xb_ref, bb_ref, cb_ref, yb_ref))
    hd = SSD_HEAD_DIM
    for gg in range(groups_per_step):
        g = gi * groups_per_step + gg
        shift = lax.rem(2 * half - g * SSD_HEADS_PER_GROUP, 2 * half)
        p_cols = pltpu.roll(col_ref[0], shift, 1)
        ep_cols = pltpu.roll(col_ref[1], shift, 1)
        for d, (x_ref, b_ref, c_ref, y_ref) in enumerate(dirs):
            bg = b_ref[:, gg * SSD_D_STATE:(gg + 1) * SSD_D_STATE]
            cg = c_ref[:, gg * SSD_D_STATE:(gg + 1) * SSD_D_STATE]
            cb_masked = jnp.where(masks[d], _dot_nt(cg, bg), 0.0)
            cg32 = cg.astype(F32)
            bt = bg.astype(F32).T
            r0 = pl.multiple_of(d * half + g * SSD_HEADS_PER_GROUP, SSD_HEADS_PER_GROUP)
            p_rows = row_ref[0, pl.ds(r0, SSD_HEADS_PER_GROUP), :]
            dt_rows = row_ref[1, pl.ds(r0, SSD_HEADS_PER_GROUP), :]
            w_rows = row_ref[2, pl.ds(r0, SSD_HEADS_PER_GROUP), :]
            dec_rows = row_ref[3, pl.ds(r0, SSD_HEADS_PER_GROUP), :]
            for hh in range(SSD_HEADS_PER_GROUP):
                li = d * half + hh
                lo = (gg * SSD_HEADS_PER_GROUP + hh) * hd
                seg = p_cols[:, li:li + 1] - p_rows[hh:hh + 1, :]
                l_mat = jnp.exp(jnp.minimum(seg, 0.0)) * (cb_masked * dt_rows[hh:hh + 1, :])
                c_scaled = cg32 * ep_cols[:, li:li + 1]
                lhs = jnp.concatenate([l_mat.astype(BF16), c_scaled.astype(BF16)], axis=1)
                xh = x_ref[:, lo:lo + hd]
                sh = st_ref[d, g, :, hh * hd:(hh + 1) * hd]
                rhs = jnp.concatenate([xh, sh.astype(BF16)], axis=0)
                y_ref[:, lo:lo + hd] = _dot(lhs, rhs).astype(y_ref.dtype)
                btw = (bt * w_rows[hh:hh + 1, :]).astype(BF16)
                st_ref[d, g, :, hh * hd:(hh + 1) * hd] = sh * dec_rows[hh:hh + 1, 0:hd] + _dot(btw, xh)


def _ssd_scan(act, small, ssd_prm, batch, seq):
    t = act.shape[0]
    q = SSD_CHUNK
    nc = seq // q
    gps = 2
    xw = gps * SSD_GROUP_WIDTH
    bw = gps * SSD_D_STATE
    b0 = ACT_B // bw
    c0 = ACT_C // bw
    dtc = SMALL_DT // LANE

    def fwd(col0):
        return lambda b, k, gi: (b * nc + k, col0 + gi)

    def bwd(col0):
        return lambda b, k, gi: (b * nc + nc - 1 - k, col0 + gi)

    blocks = 2 * [((q, xw), BF16), ((q, bw), BF16), ((q, bw), BF16), ((q, LANE), F32), ((q, xw), BF16)] + [((8, LANE), F32)]
    scratch = [
        pltpu.VMEM((2, SSD_N_GROUPS, SSD_D_STATE, SSD_GROUP_WIDTH), F32),
        pltpu.VMEM((2, q, LANE), F32),
        pltpu.VMEM((4, LANE, q), F32),
    ]
    scratch_bytes = _nbytes((2, SSD_N_GROUPS, SSD_D_STATE, SSD_GROUP_WIDTH), F32) + 6 * _nbytes((q, LANE), F32)
    return pl.pallas_call(
        functools.partial(_ssd_kernel, groups_per_step=gps),
        grid=(batch, nc, SSD_N_GROUPS // gps),
        in_specs=[
            pl.BlockSpec((q, xw), fwd(0)), pl.BlockSpec((q, bw), fwd(b0)), pl.BlockSpec((q, bw), fwd(c0)),
            pl.BlockSpec((q, LANE), lambda b, k, gi: (b * nc + k, dtc)),
            pl.BlockSpec((q, xw), bwd(0)), pl.BlockSpec((q, bw), bwd(b0)), pl.BlockSpec((q, bw), bwd(c0)),
            pl.BlockSpec((q, LANE), lambda b, k, gi: (b * nc + nc - 1 - k, dtc)),
            pl.BlockSpec((8, LANE), lambda b, k, gi: (0, 0)),
        ],
        out_specs=[pl.BlockSpec((q, xw), fwd(0)), pl.BlockSpec((q, xw), bwd(0))],
        out_shape=[jax.ShapeDtypeStruct((t, SSD_D_INNER), BF16)] * 2,
        scratch_shapes=scratch,
        compiler_params=_params(("arbitrary",) * 3, blocks, scratch_bytes, temp_bytes=64 * _nbytes((q, q), F32)),
        name="ssd_scan",
    )(act, act, act, small, act, act, act, small, ssd_prm)


def _ssd_finish_kernel(yf_ref, yb_ref, xs_ref, z_ref, prm_ref, o_ref):
    z = z_ref[...].astype(F32)
    y = yf_ref[...].astype(F32) + yb_ref[...].astype(F32) + prm_ref[0:1, :] * xs_ref[...].astype(F32)
    y = y * (z * jax.nn.sigmoid(z))
    gw = SSD_GROUP_WIDTH
    for g in range(SSD_N_GROUPS):
        yg = y[:, g * gw:(g + 1) * gw]
        ms = jnp.mean(yg * yg, axis=-1, keepdims=True)
        o_ref[:, g * gw:(g + 1) * gw] = (yg * lax.rsqrt(ms + RMS_EPS) * prm_ref[1:2, g * gw:(g + 1) * gw]).astype(o_ref.dtype)


def _ssd_finish(yf, yb, act, main, fin_prm):
    t = yf.shape[0]
    ts = _tile(t, 256)
    di = SSD_D_INNER
    row_spec = pl.BlockSpec((ts, di), lambda i: (i, 0))
    blocks = 5 * [((ts, di), BF16)] + [((8, di), F32)]
    return pl.pallas_call(
        _ssd_finish_kernel,
        grid=(t // ts,),
        in_specs=[row_spec, row_spec, row_spec, row_spec, pl.BlockSpec((8, di), lambda i: (0, 0))],
        out_specs=row_spec,
        out_shape=jax.ShapeDtypeStruct((t, di), BF16),
        compiler_params=_params(("arbitrary",), blocks, temp_bytes=4 * _nbytes((ts, di), F32)),
        name="ssd_finish",
    )(yf, yb, act, main, fin_prm)


def _rope_lanes(t, tab):
    return (t * tab[:, 0:LANE] + pltpu.roll(t, LANE - MLA_ROPE_DIM // 2, 1) * tab[:, LANE:2 * LANE]
            + pltpu.roll(t, MLA_ROPE_DIM // 2, 1) * tab[:, 2 * LANE:3 * LANE])


def _rms_rows(x, w):
    return x * lax.rsqrt(jnp.mean(x * x, axis=-1, keepdims=True) + RMS_EPS) * w


def _qproj_kernel(x_ref, nw_ref, w_ref, tab_ref, q_ref, *, scale):
    xn = _rms_rows(x_ref[...], nw_ref[0:1, :]).astype(BF16)
    tab = tab_ref[...]
    for h in range(MLA_N_HEADS):
        acc = _dot(xn, w_ref[:, h * MLA_QK_PAD:(h + 1) * MLA_QK_PAD])
        q_ref[0, h, :, 0:LANE] = (acc[:, 0:LANE] * scale).astype(q_ref.dtype)
        q_ref[0, h, :, LANE:2 * LANE] = (_rope_lanes(acc[:, LANE:2 * LANE], tab) * scale).astype(q_ref.dtype)


def _kvproj_kernel(x_ref, kr_ref, nw_ref, w_ref, tab_ref, k_ref, v_ref):
    xn = _rms_rows(x_ref[...], nw_ref[0:1, :]).astype(BF16)
    k_rope = _rope_lanes(kr_ref[...], tab_ref[...]).astype(k_ref.dtype)
    hw = MLA_NOPE_DIM + MLA_V_DIM
    for h in range(MLA_N_HEADS):
        acc = _dot(xn, w_ref[:, h * hw:(h + 1) * hw])
        k_ref[0, h, :, 0:LANE] = acc[:, 0:MLA_NOPE_DIM].astype(k_ref.dtype)
        k_ref[0, h, :, LANE:2 * LANE] = k_rope
        v_ref[0, h] = acc[:, MLA_NOPE_DIM:hw].astype(v_ref.dtype)


def _mla_proj(small, q_nw, wq, kv_nw, wkv, rope_tab, batch, seq):
    t = small.shape[0]
    tm = _tile(seq, 512)
    nsb = seq // tm
    nh = MLA_N_HEADS
    tab_spec = pl.BlockSpec((tm, 3 * LANE), lambda i: (i % nsb, 0))
    head_out = lambda width: pl.BlockSpec((1, nh, tm, width), lambda i: (i // nsb, 0, i % nsb, 0))
    nw_spec = pl.BlockSpec((8, MLA_Q_RANK), lambda i: (0, 0))
    w_spec = pl.BlockSpec((MLA_Q_RANK, nh * MLA_QK_PAD), lambda i: (0, 0))
    x_blocks = [((tm, MLA_Q_RANK), F32), ((8, MLA_Q_RANK), F32), ((MLA_Q_RANK, nh * MLA_QK_PAD), BF16), ((tm, 3 * LANE), F32)]
    q = pl.pallas_call(
        functools.partial(_qproj_kernel, scale=MLA_QK_DIM ** -0.5),
        grid=(t // tm,),
        in_specs=[pl.BlockSpec((tm, MLA_Q_RANK), lambda i: (i, SMALL_Q // MLA_Q_RANK)), nw_spec, w_spec, tab_spec],
        out_specs=head_out(MLA_QK_PAD),
        out_shape=jax.ShapeDtypeStruct((batch, nh, seq, MLA_QK_PAD), BF16),
        compiler_params=_params(("arbitrary",), x_blocks + [((nh, tm, MLA_QK_PAD), BF16)], temp_bytes=8 * _nbytes((tm, MLA_QK_PAD), F32)),
        name="mla_q_proj",
    )(small, q_nw, wq, rope_tab)
    k, v = pl.pallas_call(
        _kvproj_kernel,
        grid=(t // tm,),
        in_specs=[pl.BlockSpec((tm, MLA_KV_RANK), lambda i: (i, SMALL_CKV // MLA_KV_RANK)),
                  pl.BlockSpec((tm, LANE), lambda i: (i, SMALL_KROPE // LANE)), nw_spec, w_spec, tab_spec],
        out_specs=[head_out(MLA_QK_PAD), head_out(MLA_V_DIM)],
        out_shape=[jax.ShapeDtypeStruct((batch, nh, seq, MLA_QK_PAD), BF16), jax.ShapeDtypeStruct((batch, nh, seq, MLA_V_DIM), BF16)],
        compiler_params=_params(("arbitrary",), x_blocks + [((tm, LANE), F32), ((nh, tm, MLA_QK_PAD), BF16), ((nh, tm, MLA_V_DIM), BF16)],
                                temp_bytes=8 * _nbytes((tm, MLA_QK_PAD), F32)),
        name="mla_kv_proj",
    )(small, small, kv_nw, wkv, rope_tab)
    return q, k, v


def _flash_kernel(q_ref, k_ref, v_ref, o_ref, *, tk):
    q = q_ref[0, 0]
    tq = q.shape[0]
    n_kv = k_ref.shape[2] // tk

    def body(j, carry):
        m, l, acc = carry
        start = pl.multiple_of(j * tk, tk)
        s = _dot_nt(q, k_ref[0, 0, pl.ds(start, tk), :])
        m_new = jnp.maximum(m, jnp.max(s, axis=-1, keepdims=True))
        alpha = jnp.exp(m - m_new)
        p = jnp.exp(s - m_new)
        l = alpha * l + jnp.sum(p, axis=-1, keepdims=True)
        acc = alpha * acc + _dot(p.astype(BF16), v_ref[0, 0, pl.ds(start, tk), :])
        return m_new, l, acc

    init = (jnp.full((tq, 1), -jnp.inf, F32), jnp.zeros((tq, 1), F32), jnp.zeros((tq, MLA_V_DIM), F32))
    _, l, acc = lax.fori_loop(0, n_kv, body, init)
    o_ref[...] = (acc / l).astype(o_ref.dtype)


def _flash_attention(q, k, v):
    batch, nh, seq, _ = q.shape
    tq = _tile(seq, 512)
    tk = _tile(seq, 512)
    nq = seq // tq
    blocks = [((tq, MLA_QK_PAD), BF16), ((seq, MLA_QK_PAD), BF16), ((seq, MLA_V_DIM), BF16), ((tq, MLA_V_DIM), BF16)]
    return pl.pallas_call(
        functools.partial(_flash_kernel, tk=tk),
        grid=(batch, nh, nq),
        in_specs=[
            pl.BlockSpec((1, 1, tq, MLA_QK_PAD), lambda b, h, i: (b, h, i, 0)),
            pl.BlockSpec((1, 1, seq, MLA_QK_PAD), lambda b, h, i: (b, h, 0, 0)),
            pl.BlockSpec((1, 1, seq, MLA_V_DIM), lambda b, h, i: (b, h, 0, 0)),
        ],
        out_specs=pl.BlockSpec((tq, MLA_V_DIM), lambda b, h, i: (b * nq + i, h)),
        out_shape=jax.ShapeDtypeStruct((batch * seq, nh * MLA_V_DIM), BF16),
        compiler_params=_params(("arbitrary",) * 3, blocks, temp_bytes=6 * _nbytes((tq, tk), F32)),
        name="mla_flash_attention",
    )(q, k, v)


def _merge_kernel(a1_ref, w1_ref, a2_ref, w2_ref, g1_ref, g2_ref, o_ref):
    y1 = _dot(a1_ref[...], w1_ref[...])
    y2 = _dot(a2_ref[...], w2_ref[...])
    g1 = jax.nn.sigmoid(g1_ref[...].astype(F32))
    g2 = jax.nn.sigmoid(g2_ref[...].astype(F32))
    o_ref[...] = (g1 * y1 + g2 * y2).astype(o_ref.dtype)


def _merge(y_ssd_n, w_ssd_out, attn, w_mla_out, main):
    t = y_ssd_n.shape[0]
    tm = _tile(t, 512)
    tn = 512
    k1, k2 = SSD_D_INNER, MLA_N_HEADS * MLA_V_DIM
    blocks = [((tm, k1), BF16), ((k1, tn), BF16), ((tm, k2), BF16), ((k2, tn), BF16), ((tm, tn), BF16), ((tm, tn), BF16), ((tm, tn), BF16)]
    return pl.pallas_call(
        _merge_kernel,
        grid=(D_MODEL // tn, t // tm),
        in_specs=[
            pl.BlockSpec((tm, k1), lambda j, i: (i, 0)), pl.BlockSpec((k1, tn), lambda j, i: (0, j)),
            pl.BlockSpec((tm, k2), lambda j, i: (i, 0)), pl.BlockSpec((k2, tn), lambda j, i: (0, j)),
            pl.BlockSpec((tm, tn), lambda j, i: (i, MAIN_GSSD // tn + j)),
            pl.BlockSpec((tm, tn), lambda j, i: (i, MAIN_GMLA // tn + j)),
        ],
        out_specs=pl.BlockSpec((tm, tn), lambda j, i: (i, j)),
        out_shape=jax.ShapeDtypeStruct((t, D_MODEL), BF16),
        compiler_params=_params(("arbitrary",) * 2, blocks, temp_bytes=4 * _nbytes((tm, tn), F32)),
        name="branch_merge",
    )(y_ssd_n, w_ssd_out, attn, w_mla_out, main, main)


def _layer_norm_rows(r, prm_ref):
    mu = jnp.mean(r, axis=-1, keepdims=True)
    c = r - mu
    var = jnp.mean(c * c, axis=-1, keepdims=True)
    return c * lax.rsqrt(var + LN_EPS) * prm_ref[0:1, :] + prm_ref[1:2, :]


def _out_ln_kernel(a_ref, w_ref, h_ref, prm_ref, of_ref, ob_ref):
    y = _layer_norm_rows(DEEPNORM_ALPHA * h_ref[...] + _dot(a_ref[...], w_ref[...]), prm_ref)
    of_ref[...] = y
    ob_ref[...] = y.astype(ob_ref.dtype)


def _out_ln(merged, w_out, h, ln_prm):
    t = h.shape[0]
    tm = _tile(t, 256)
    d = D_MODEL
    blocks = [((tm, d), BF16), ((d, d), BF16), ((tm, d), F32), ((8, d), F32), ((tm, d), F32), ((tm, d), BF16)]
    row = lambda i: (i, 0)
    return pl.pallas_call(
        _out_ln_kernel,
        grid=(t // tm,),
        in_specs=[pl.BlockSpec((tm, d), row), pl.BlockSpec((d, d), lambda i: (0, 0)), pl.BlockSpec((tm, d), row),
                  pl.BlockSpec((8, d), lambda i: (0, 0))],
        out_specs=[pl.BlockSpec((tm, d), row), pl.BlockSpec((tm, d), row)],
        out_shape=[jax.ShapeDtypeStruct((t, d), F32), jax.ShapeDtypeStruct((t, d), BF16)],
        compiler_params=_params(("arbitrary",), blocks, temp_bytes=3 * _nbytes((tm, d), F32)),
        name="mixer_out_ln",
    )(merged, w_out, h, ln_prm)


def _ffn_up_kernel(a_ref, wg_ref, wu_ref, o_ref):
    a = a_ref[...]
    gate = _dot(a, wg_ref[...])
    o_ref[...] = (gate * jax.nn.sigmoid(gate) * _dot(a, wu_ref[...])).astype(o_ref.dtype)


def _ffn_up(hb, wg, wu):
    t = hb.shape[0]
    tm = _tile(t, 1024)
    tn = 512
    d = D_MODEL
    blocks = [((tm, d), BF16), ((d, tn), BF16), ((d, tn), BF16), ((tm, tn), BF16)]
    return pl.pallas_call(
        _ffn_up_kernel,
        grid=(FFN_HIDDEN // tn, t // tm),
        in_specs=[pl.BlockSpec((tm, d), lambda j, i: (i, 0)), pl.BlockSpec((d, tn), lambda j, i: (0, j)),
                  pl.BlockSpec((d, tn), lambda j, i: (0, j))],
        out_specs=pl.BlockSpec((tm, tn), lambda j, i: (i, j)),
        out_shape=jax.ShapeDtypeStruct((t, FFN_HIDDEN), BF16),
        compiler_params=_params(("arbitrary",) * 2, blocks, temp_bytes=3 * _nbytes((tm, tn), F32)),
        name="ffn_up",
    )(hb, wg, wu)


def _ffn_down_ln_kernel(a_ref, w_ref, h_ref, prm_ref, of_ref, ob_ref, acc_ref):
    kk = pl.program_id(1)

    @pl.when(kk == 0)
    def _init():
        acc_ref[...] = DEEPNORM_ALPHA * h_ref[...]

    acc_ref[...] += _dot(a_ref[...], w_ref[...])

    @pl.when(kk == pl.num_programs(1) - 1)
    def _finish():
        y = _layer_norm_rows(acc_ref[...], prm_ref)
        of_ref[...] = y
        ob_ref[...] = y.astype(ob_ref.dtype)


def _ffn_down_ln(act, w_down, h, ln_prm):
    t = h.shape[0]
    tm = _tile(t, 512)
    tk = 512
    d = D_MODEL
    blocks = [((tm, tk), BF16), ((tk, d), BF16), ((tm, d), F32), ((8, d), F32), ((tm, d), F32), ((tm, d), BF16)]
    row = lambda i, kk: (i, 0)
    return pl.pallas_call(
        _ffn_down_ln_kernel,
        grid=(t // tm, FFN_HIDDEN // tk),
        in_specs=[pl.BlockSpec((tm, tk), lambda i, kk: (i, kk)), pl.BlockSpec((tk, d), lambda i, kk: (kk, 0)),
                  pl.BlockSpec((tm, d), row), pl.BlockSpec((8, d), lambda i, kk: (0, 0))],
        out_specs=[pl.BlockSpec((tm, d), row), pl.BlockSpec((tm, d), row)],
        out_shape=[jax.ShapeDtypeStruct((t, d), F32), jax.ShapeDtypeStruct((t, d), BF16)],
        scratch_shapes=[pltpu.VMEM((tm, d), F32)],
        compiler_params=_params(("arbitrary",) * 2, blocks, _nbytes((tm, d), F32), temp_bytes=2 * _nbytes((tm, d), F32)),
        name="ffn_down_ln",
    )(act, w_down, h, ln_prm)


def _rows8(*rows):
    n = rows[0].shape[0]
    pad = jnp.zeros((8 - len(rows), n), F32)
    return jnp.concatenate([jnp.stack([r.astype(F32) for r in rows]), pad], axis=0)


def _rope_table(seq):
    pos = jnp.arange(seq, dtype=F32)
    inv_freq = ROPE_THETA ** (-jnp.arange(0, MLA_ROPE_DIM, 2, dtype=F32) / MLA_ROPE_DIM)
    ang = pos[:, None] * inv_freq[None, :]
    cos, sin = jnp.cos(ang), jnp.sin(ang)
    half = MLA_ROPE_DIM // 2
    z = lambda n: jnp.zeros((seq, n), F32)
    return jnp.concatenate([cos, cos, z(LANE - 2 * half), -sin, z(LANE - half), z(half), sin, z(LANE - 2 * half)], axis=1)


def _pad_q_weight(w_uq):
    r = w_uq.shape[0]
    w = w_uq.reshape(r, MLA_N_HEADS, MLA_QK_DIM)
    w = jnp.pad(w, ((0, 0), (0, 0), (0, MLA_QK_PAD - MLA_QK_DIM)))
    return w.reshape(r, MLA_N_HEADS * MLA_QK_PAD).astype(BF16)


def kernel(x, w_in, conv_w, conv_b, ssd_a_log, ssd_dt_bias, ssd_d, ssd_norm_w, w_ssd_out, mla_q_norm_w, w_uq,
           mla_kv_norm_w, w_ukv, w_mla_out, w_out, ln1_g, ln1_b, w_ffn_gate, w_ffn_up, w_ffn_down, ln2_g, ln2_b):
    batch, seq, d = x.shape
    assert d == D_MODEL and seq % SSD_CHUNK == 0
    t = batch * seq
    hf = x.reshape(t, d)
    hb = hf.astype(BF16)
    rope_tab = _rope_table(seq)
    dt0 = MAIN_GSSD + 2 * SSD_N_HEADS
    q0 = dt0
    kv0 = q0 + MLA_Q_RANK
    kr0 = kv0 + MLA_KV_RANK
    g0 = kr0 + MLA_ROPE_DIM
    for l in range(w_in.shape[0]):
        wi = w_in[l]
        w_main = jnp.concatenate([wi[:, :MAIN_GSSD], wi[:, g0:]], axis=1).astype(BF16)
        w_small = jnp.concatenate(
            [wi[:, q0:kv0], wi[:, kv0:kr0], wi[:, MAIN_GSSD:dt0], wi[:, kr0:g0], jnp.zeros((d, LANE - MLA_ROPE_DIM), wi.dtype)],
            axis=1).astype(BF16)
        main = _matmul(hb, w_main, BF16, 1024, "in_proj_main")
        small = _matmul(hb, w_small, F32, SMALL_WIDTH, "in_proj_small")

        act = _conv_silu(main, _rows8(*[conv_w[l, i] for i in range(SSD_CONV_WIDTH)], conv_b[l]), batch, seq)
        yf, yb = _ssd_scan(act, small, _rows8(ssd_dt_bias[l].reshape(-1), ssd_a_log[l].reshape(-1)), batch, seq)
        y_ssd_n = _ssd_finish(yf, yb, act, main, _rows8(jnp.repeat(ssd_d[l], SSD_HEAD_DIM), ssd_norm_w[l]))

        q, k, v = _mla_proj(small, _rows8(mla_q_norm_w[l]), _pad_q_weight(w_uq[l]), _rows8(mla_kv_norm_w[l]),
                            w_ukv[l].astype(BF16), rope_tab, batch, seq)
        attn = _flash_attention(q, k, v)

        merged = _merge(y_ssd_n, w_ssd_out[l].astype(BF16), attn, w_mla_out[l].astype(BF16), main)
        hf, hb = _out_ln(merged, w_out[l].astype(BF16), hf, _rows8(ln1_g[l], ln1_b[l]))
        ffn_act = _ffn_up(hb, w_ffn_gate[l].astype(BF16), w_ffn_up[l].astype(BF16))
        hf, hb = _ffn_down_ln(ffn_act, w_ffn_down[l].astype(BF16), hf, _rows8(ln2_g[l], ln2_b[l]))
    return hf.reshape(batch, seq, d)
```

```python
import functools

import jax
import jax.numpy as jnp
from jax import lax
from jax.experimental import pallas as pl
from jax.experimental.pallas import tpu as pltpu

F32 = jnp.float32
BF16 = jnp.bfloat16

D_MODEL = 2048
DEPTH = 4
SSD_D_INNER = 4096
SSD_HEAD_DIM = 64
SSD_N_HEADS = 64
SSD_N_GROUPS = 8
SSD_HEADS_PER_GROUP = 8
SSD_D_STATE = 128
SSD_CONV_WIDTH = 5
SSD_CHUNK = 128
SSD_GROUP_WIDTH = SSD_HEADS_PER_GROUP * SSD_HEAD_DIM
SSD_BC_WIDTH = SSD_N_GROUPS * SSD_D_STATE
SSD_CONV_DIM = SSD_D_INNER + 2 * SSD_BC_WIDTH
MLA_N_HEADS = 16
MLA_Q_RANK = 512
MLA_KV_RANK = 512
MLA_NOPE_DIM = 128
MLA_ROPE_DIM = 64
MLA_V_DIM = 128
MLA_QK_DIM = MLA_NOPE_DIM + MLA_ROPE_DIM
MLA_QK_PAD = 256
ROPE_THETA = 10000.0
FLASH_TQ = 512
FLASH_TK = 256
FLASH_VT_ROWS = MLA_V_DIM + 16
FLASH_UNROLL = 2
LOG2_E = 1.4426950408889634
FFN_HIDDEN = 5632
DEEPNORM_ALPHA = (2 * DEPTH) ** 0.25
RMS_EPS = 1e-6
LN_EPS = 1e-5

MAIN_Z = 0
MAIN_XBC = SSD_D_INNER
MAIN_GSSD = MAIN_XBC + SSD_CONV_DIM
MAIN_GMLA = MAIN_GSSD + D_MODEL
MAIN_WIDTH = MAIN_GMLA + D_MODEL
SMALL_Q = 0
SMALL_CKV = MLA_Q_RANK
SMALL_DT = SMALL_CKV + MLA_KV_RANK
SMALL_KROPE = SMALL_DT + 2 * SSD_N_HEADS
SMALL_WIDTH = SMALL_KROPE + 128
ACT_B = SSD_D_INNER
ACT_C = SSD_D_INNER + SSD_BC_WIDTH

V7X_VMEM_BYTES = 64 * 1024 * 1024
VMEM_LIMIT_CAP = 56 * 1024 * 1024
VMEM_LIMIT_FLOOR = 32 * 1024 * 1024
LANE = 128
BF16_SUBLANE_TILE = 16


def _nbytes(shape, dtype):
    n = 1
    for s in shape:
        n *= s
    return n * jnp.dtype(dtype).itemsize


def _params(semantics, blocks, scratch_bytes=0, temp_bytes=0):
    need = 2 * sum(_nbytes(s, d) for s, d in blocks) + scratch_bytes + temp_bytes
    limit = min(max(need + need // 4, VMEM_LIMIT_FLOOR), VMEM_LIMIT_CAP)
    return pltpu.CompilerParams(dimension_semantics=semantics, vmem_limit_bytes=limit)


def _tile(n, pref):
    t = min(n, pref)
    assert n % t == 0, (n, pref)
    return t


def _dot(a, b):
    return jnp.dot(a, b, preferred_element_type=F32)


def _dot_nt(a, b):
    return lax.dot_general(a, b, (((1,), (1,)), ((), ())), preferred_element_type=F32)


def _mm_kernel(a_ref, w_ref, o_ref):
    o_ref[...] = _dot(a_ref[...], w_ref[...]).astype(o_ref.dtype)


def _matmul(a, w, out_dtype, tn, name):
    m, k = a.shape
    n = w.shape[1]
    tm = _tile(m, 1024)
    tn = _tile(n, tn)
    blocks = [((tm, k), a.dtype), ((k, tn), w.dtype), ((tm, tn), out_dtype)]
    return pl.pallas_call(
        _mm_kernel,
        grid=(n // tn, m // tm),
        in_specs=[pl.BlockSpec((tm, k), lambda j, i: (i, 0)), pl.BlockSpec((k, tn), lambda j, i: (0, j))],
        out_specs=pl.BlockSpec((tm, tn), lambda j, i: (i, j)),
        out_shape=jax.ShapeDtypeStruct((m, n), out_dtype),
        compiler_params=_params(("arbitrary", "arbitrary"), blocks, temp_bytes=_nbytes((tm, tn), F32)),
        name=name,
    )(a, w)


def _conv_kernel(prev_ref, cur_ref, next_ref, w_ref, o_ref, *, n_seq_blocks):
    j = pl.program_id(1)
    cur = cur_ref[...].astype(F32)
    ts = cur.shape[0]
    has_prev = j > 0
    has_next = j < n_seq_blocks - 1
    halo = BF16_SUBLANE_TILE
    p0 = jnp.where(has_prev, prev_ref[halo - 2:halo - 1, :].astype(F32), 0.0)
    p1 = jnp.where(has_prev, prev_ref[halo - 1:halo, :].astype(F32), 0.0)
    n0 = jnp.where(has_next, next_ref[0:1, :].astype(F32), 0.0)
    n1 = jnp.where(has_next, next_ref[1:2, :].astype(F32), 0.0)
    row = lax.broadcasted_iota(jnp.int32, cur.shape, 0)
    xm1 = jnp.where(row == 0, p1, pltpu.roll(cur, 1, 0))
    xm2 = jnp.where(row == 0, p0, jnp.where(row == 1, p1, pltpu.roll(cur, 2, 0)))
    xp1 = jnp.where(row == ts - 1, n0, pltpu.roll(cur, ts - 1, 0))
    xp2 = jnp.where(row == ts - 2, n0, jnp.where(row == ts - 1, n1, pltpu.roll(cur, ts - 2, 0)))
    w = w_ref[...]
    y = xm2 * w[0:1] + xm1 * w[1:2] + cur * w[2:3] + xp1 * w[3:4] + xp2 * w[4:5] + w[5:6]
    o_ref[...] = (y * jax.nn.sigmoid(y)).astype(o_ref.dtype)


def _conv_silu(main, conv_prm, batch, seq):
    t = main.shape[0]
    ts = _tile(seq, 512)
    tc = 1024
    nsb = seq // ts
    col0 = MAIN_XBC // tc
    halo = BF16_SUBLANE_TILE
    rpb = ts // halo
    last_halo = t // halo - 1
    blocks = [((halo, tc), BF16), ((ts, tc), BF16), ((halo, tc), BF16), ((8, tc), F32), ((ts, tc), BF16)]
    return pl.pallas_call(
        functools.partial(_conv_kernel, n_seq_blocks=nsb),
        grid=(batch, nsb, SSD_CONV_DIM // tc),
        in_specs=[
            pl.BlockSpec((halo, tc), lambda b, j, c: (jnp.maximum((b * nsb + j) * rpb - 1, 0), col0 + c)),
            pl.BlockSpec((ts, tc), lambda b, j, c: (b * nsb + j, col0 + c)),
            pl.BlockSpec((halo, tc), lambda b, j, c: (jnp.minimum((b * nsb + j + 1) * rpb, last_halo), col0 + c)),
            pl.BlockSpec((8, tc), lambda b, j, c: (0, c)),
        ],
        out_specs=pl.BlockSpec((ts, tc), lambda b, j, c: (b * nsb + j, c)),
        out_shape=jax.ShapeDtypeStruct((t, SSD_CONV_DIM), BF16),
        compiler_params=_params(("arbitrary",) * 3, blocks, temp_bytes=8 * _nbytes((ts, tc), F32)),
        name="conv_silu",
    )(main, main, main, conv_prm)


def _ssd_kernel(xf_ref, bf_ref, cf_ref, dtf_ref, xb_ref, bb_ref, cb_ref, dtb_ref, prm_ref,
                yf_ref, yb_ref, st_ref, col_ref, row_ref, *, groups_per_step):
    q = SSD_CHUNK
    k = pl.program_id(1)
    gi = pl.program_id(2)
    half = SSD_N_HEADS

    @pl.when(jnp.logical_and(k == 0, gi == 0))
    def _zero_state():
        st_ref[...] = jnp.zeros(st_ref.shape, st_ref.dtype)

    @pl.when(gi == 0)
    def _per_chunk_decay_terms():
        lane = lax.broadcasted_iota(jnp.int32, (q, 2 * half), 1)
        is_fwd = lane < half
        x = jnp.where(is_fwd, dtf_ref[...], dtb_ref[...]) + prm_ref[0:1, :]
        dt = jnp.maximum(x, 0.0) + jnp.log1p(jnp.exp(-jnp.abs(x)))
        a = -jnp.exp(prm_ref[1:2, :]) * dt
        r = lax.broadcasted_iota(jnp.int32, (q, q), 0)
        c = lax.broadcasted_iota(jnp.int32, (q, q), 1)
        tril = (r >= c).astype(BF16)
        triu = (r <= c).astype(BF16)
        a_hi = a.astype(BF16)
        a_lo = (a - a_hi.astype(F32)).astype(BF16)
        p = jnp.where(is_fwd, _dot(tril, a_hi) + _dot(tril, a_lo), _dot(triu, a_hi) + _dot(triu, a_lo))
        p_tot = jnp.where(is_fwd[0:1], p[q - 1:q, :], p[0:1, :])
        col_ref[0] = p
        col_ref[1] = jnp.exp(p)
        row_ref[0] = p.T
        row_ref[1] = dt.T
        row_ref[2] = (dt * jnp.exp(p_tot - p)).T
        row_ref[3] = jnp.broadcast_to(jnp.exp(p_tot), (q, 2 * half)).T

    ri = lax.broadcasted_iota(jnp.int32, (q, q), 0)
    ci = lax.broadcasted_iota(jnp.int32, (q, q), 1)
    masks = (ri >= ci, ri <= ci)
    low_lanes = ci < SSD_HEAD_DIM
    dirs = ((xf_ref, bf_ref, cf_ref, yf_ref), (xb_ref, bb_ref, cb_ref, yb_ref))
    hd = SSD_HEAD_DIM
    zero_b = jnp.zeros((q, 2 * hd), BF16)
    for gg in range(groups_per_step):
        g = gi * groups_per_step + gg
        shift = lax.rem(2 * half - g * SSD_HEADS_PER_GROUP, 2 * half)
        p_cols = pltpu.roll(col_ref[0], shift, 1)
        ep_cols = pltpu.roll(col_ref[1], shift, 1)
        for d, (x_ref, b_ref, c_ref, y_ref) in enumerate(dirs):
            bg = b_ref[:, gg * SSD_D_STATE:(gg + 1) * SSD_D_STATE]
            cg = c_ref[:, gg * SSD_D_STATE:(gg + 1) * SSD_D_STATE]
            cb_masked = jnp.where(masks[d], _dot_nt(cg, bg), 0.0)
            cg32 = cg.astype(F32)
            bt = bg.astype(F32).T
            r0 = pl.multiple_of(d * half + g * SSD_HEADS_PER_GROUP, SSD_HEADS_PER_GROUP)
            p_rows = row_ref[0, pl.ds(r0, SSD_HEADS_PER_GROUP), :]
            dt_rows = row_ref[1, pl.ds(r0, SSD_HEADS_PER_GROUP), :]
            w_rows = row_ref[2, pl.ds(r0, SSD_HEADS_PER_GROUP), :]
            dec_rows = row_ref[3, pl.ds(r0, SSD_HEADS_PER_GROUP), :]

            def head_terms(hh):
                li = d * half + hh
                seg = p_cols[:, li:li + 1] - p_rows[hh:hh + 1, :]
                l_mat = jnp.exp(jnp.minimum(seg, 0.0)) * (cb_masked * dt_rows[hh:hh + 1, :])
                c_scaled = cg32 * ep_cols[:, li:li + 1]
                btw = bt * w_rows[hh:hh + 1, :]
                return l_mat.astype(BF16), c_scaled.astype(BF16), btw.astype(BF16)

            for hp in range(SSD_HEADS_PER_GROUP // 2):
                h0, h1 = 2 * hp, 2 * hp + 1
                lo = (gg * SSD_HEADS_PER_GROUP + h0) * hd
                xp = x_ref[:, lo:lo + 2 * hd]
                x0 = jnp.where(low_lanes, xp, zero_b)
                x1 = jnp.where(low_lanes, zero_b, xp)
                sp = st_ref[d, g, :, h0 * hd:(h0 + 2) * hd]
                spb = sp.astype(BF16)
                l0, c0, w0 = head_terms(h0)
                l1, c1, w1 = head_terms(h1)
                lhs = jnp.concatenate([l0, l1, c0, c1], axis=1)
                x_rhs = jnp.concatenate([x0, x1], axis=0)
                rhs = jnp.concatenate([x_rhs, jnp.where(low_lanes, spb, zero_b), jnp.where(low_lanes, zero_b, spb)], axis=0)
                y_ref[:, lo:lo + 2 * hd] = _dot(lhs, rhs).astype(y_ref.dtype)
                dec = jnp.where(low_lanes[0:1], dec_rows[h0:h0 + 1, :], dec_rows[h1:h1 + 1, :])
                st_ref[d, g, :, h0 * hd:(h0 + 2) * hd] = sp * dec + _dot(jnp.concatenate([w0, w1], axis=1), x_rhs)


def _ssd_scan(act, small, ssd_prm, batch, seq):
    t = act.shape[0]
    q = SSD_CHUNK
    nc = seq // q
    gps = 2
    xw = gps * SSD_GROUP_WIDTH
    bw = gps * SSD_D_STATE
    b0 = ACT_B // bw
    c0 = ACT_C // bw
    dtc = SMALL_DT // LANE

    def fwd(col0):
        return lambda b, k, gi: (b * nc + k, col0 + gi)

    def bwd(col0):
        return lambda b, k, gi: (b * nc + nc - 1 - k, col0 + gi)

    blocks = 2 * [((q, xw), BF16), ((q, bw), BF16), ((q, bw), BF16), ((q, LANE), F32), ((q, xw), BF16)] + [((8, LANE), F32)]
    scratch = [
        pltpu.VMEM((2, SSD_N_GROUPS, SSD_D_STATE, SSD_GROUP_WIDTH), F32),
        pltpu.VMEM((2, q, LANE), F32),
        pltpu.VMEM((4, LANE, q), F32),
    ]
    scratch_bytes = _nbytes((2, SSD_N_GROUPS, SSD_D_STATE, SSD_GROUP_WIDTH), F32) + 6 * _nbytes((q, LANE), F32)
    return pl.pallas_call(
        functools.partial(_ssd_kernel, groups_per_step=gps),
        grid=(batch, nc, SSD_N_GROUPS // gps),
        in_specs=[
            pl.BlockSpec((q, xw), fwd(0)), pl.BlockSpec((q, bw), fwd(b0)), pl.BlockSpec((q, bw), fwd(c0)),
            pl.BlockSpec((q, LANE), lambda b, k, gi: (b * nc + k, dtc)),
            pl.BlockSpec((q, xw), bwd(0)), pl.BlockSpec((q, bw), bwd(b0)), pl.BlockSpec((q, bw), bwd(c0)),
            pl.BlockSpec((q, LANE), lambda b, k, gi: (b * nc + nc - 1 - k, dtc)),
            pl.BlockSpec((8, LANE), lambda b, k, gi: (0, 0)),
        ],
        out_specs=[pl.BlockSpec((q, xw), fwd(0)), pl.BlockSpec((q, xw), bwd(0))],
        out_shape=[jax.ShapeDtypeStruct((t, SSD_D_INNER), BF16)] * 2,
        scratch_shapes=scratch,
        compiler_params=_params(("arbitrary",) * 3, blocks, scratch_bytes, temp_bytes=64 * _nbytes((q, q), F32)),
        name="ssd_scan",
    )(act, act, act, small, act, act, act, small, ssd_prm)


def _ssd_finish_kernel(yf_ref, yb_ref, xs_ref, z_ref, prm_ref, o_ref):
    z = z_ref[...].astype(F32)
    y = yf_ref[...].astype(F32) + yb_ref[...].astype(F32) + prm_ref[0:1, :] * xs_ref[...].astype(F32)
    y = y * (z * jax.nn.sigmoid(z))
    gw = SSD_GROUP_WIDTH
    for g in range(SSD_N_GROUPS):
        yg = y[:, g * gw:(g + 1) * gw]
        ms = jnp.mean(yg * yg, axis=-1, keepdims=True)
        o_ref[:, g * gw:(g + 1) * gw] = (yg * lax.rsqrt(ms + RMS_EPS) * prm_ref[1:2, g * gw:(g + 1) * gw]).astype(o_ref.dtype)


def _ssd_finish(yf, yb, act, main, fin_prm):
    t = yf.shape[0]
    ts = _tile(t, 256)
    di = SSD_D_INNER
    row_spec = pl.BlockSpec((ts, di), lambda i: (i, 0))
    blocks = 5 * [((ts, di), BF16)] + [((8, di), F32)]
    return pl.pallas_call(
        _ssd_finish_kernel,
        grid=(t // ts,),
        in_specs=[row_spec, row_spec, row_spec, row_spec, pl.BlockSpec((8, di), lambda i: (0, 0))],
        out_specs=row_spec,
        out_shape=jax.ShapeDtypeStruct((t, di), BF16),
        compiler_params=_params(("arbitrary",), blocks, temp_bytes=4 * _nbytes((ts, di), F32)),
        name="ssd_finish",
    )(yf, yb, act, main, fin_prm)


def _rope_lanes(t, tab):
    return (t * tab[:, 0:LANE] + pltpu.roll(t, LANE - MLA_ROPE_DIM // 2, 1) * tab[:, LANE:2 * LANE]
            + pltpu.roll(t, MLA_ROPE_DIM // 2, 1) * tab[:, 2 * LANE:3 * LANE])


def _rms_rows(x, w):
    return x * lax.rsqrt(jnp.mean(x * x, axis=-1, keepdims=True) + RMS_EPS) * w


def _qproj_kernel(x_ref, nw_ref, w_ref, tab_ref, qt_ref, *, scale):
    xt = x_ref[...].T
    tm = xt.shape[1]
    inv = lax.rsqrt(jnp.mean(xt * xt, axis=0, keepdims=True) + RMS_EPS)
    nw = jnp.concatenate([nw_ref[...]] * (tm // LANE), axis=1)
    xnt = (xt * inv * nw).astype(BF16)
    half = MLA_ROPE_DIM // 2
    cos = tab_ref[0:half, :]
    sin = tab_ref[half:2 * half, :]
    r0 = MLA_NOPE_DIM
    for h in range(MLA_N_HEADS):
        qt = _dot(w_ref[h * MLA_QK_PAD:(h + 1) * MLA_QK_PAD, :], xnt)
        x1 = qt[r0:r0 + half]
        x2 = qt[r0 + half:r0 + 2 * half]
        qt_ref[0, h, 0:r0, :] = (qt[0:r0] * scale).astype(qt_ref.dtype)
        qt_ref[0, h, r0:r0 + half, :] = ((x1 * cos - x2 * sin) * scale).astype(qt_ref.dtype)
        qt_ref[0, h, r0 + half:r0 + 2 * half, :] = ((x2 * cos + x1 * sin) * scale).astype(qt_ref.dtype)
        qt_ref[0, h, r0 + 2 * half:MLA_QK_PAD, :] = jnp.zeros((MLA_QK_PAD - MLA_QK_DIM, tm), qt_ref.dtype)


def _kvproj_kernel(x_ref, kr_ref, nw_ref, wk_ref, wvt_ref, tab_ref, k_ref, vt_ref, *, tk):
    cn = _rms_rows(x_ref[...], nw_ref[0:1, :])
    tm = cn.shape[0]
    k_all = _dot(cn.astype(BF16), wk_ref[...])
    vt_all = _dot(wvt_ref[...], cn.T.astype(BF16))
    k_rope = _rope_lanes(kr_ref[...], tab_ref[...]).astype(k_ref.dtype)
    pad_rows = FLASH_VT_ROWS - MLA_V_DIM
    ones_rows = (lax.broadcasted_iota(jnp.int32, (pad_rows, tk), 0) == 0).astype(vt_ref.dtype)
    for h in range(MLA_N_HEADS):
        k_ref[0, h, :, 0:LANE] = k_all[:, h * MLA_NOPE_DIM:(h + 1) * MLA_NOPE_DIM].astype(k_ref.dtype)
        k_ref[0, h, :, LANE:2 * LANE] = k_rope
        for c in range(tm // tk):
            vt_ref[0, h, c, 0:MLA_V_DIM, :] = vt_all[h * MLA_V_DIM:(h + 1) * MLA_V_DIM, c * tk:(c + 1) * tk].astype(vt_ref.dtype)
            vt_ref[0, h, c, MLA_V_DIM:FLASH_VT_ROWS, :] = ones_rows


def _mla_proj(small, q_nw_lanes, wq_t, kv_nw, wk, wv_t, rope_tab, rope_tab_t, batch, seq):
    t = small.shape[0]
    tk = _tile(seq, FLASH_TK)
    tm = _tile(seq, max(512, tk))
    nsb = seq // tm
    nh = MLA_N_HEADS
    qw = nh * MLA_QK_PAD
    kw = nh * MLA_NOPE_DIM
    vw = nh * MLA_V_DIM
    q_blocks = [((tm, MLA_Q_RANK), F32), ((MLA_Q_RANK, LANE), F32), ((qw, MLA_Q_RANK), BF16), ((MLA_ROPE_DIM, tm), F32),
                ((nh, MLA_QK_PAD, tm), BF16)]
    qt = pl.pallas_call(
        functools.partial(_qproj_kernel, scale=MLA_QK_DIM ** -0.5 * LOG2_E),
        grid=(t // tm,),
        in_specs=[pl.BlockSpec((tm, MLA_Q_RANK), lambda i: (i, SMALL_Q // MLA_Q_RANK)),
                  pl.BlockSpec((MLA_Q_RANK, LANE), lambda i: (0, 0)),
                  pl.BlockSpec((qw, MLA_Q_RANK), lambda i: (0, 0)),
                  pl.BlockSpec((MLA_ROPE_DIM, tm), lambda i: (0, i % nsb))],
        out_specs=pl.BlockSpec((1, nh, MLA_QK_PAD, tm), lambda i: (i // nsb, 0, 0, i % nsb)),
        out_shape=jax.ShapeDtypeStruct((batch, nh, MLA_QK_PAD, seq), BF16),
        compiler_params=_params(("arbitrary",), q_blocks, temp_bytes=6 * _nbytes((tm, MLA_Q_RANK), F32)),
        name="mla_q_proj",
    )(small, q_nw_lanes, wq_t, rope_tab_t)
    kv_blocks = [((tm, MLA_KV_RANK), F32), ((tm, LANE), F32), ((8, MLA_KV_RANK), F32), ((MLA_KV_RANK, kw), BF16),
                 ((vw, MLA_KV_RANK), BF16), ((tm, 3 * LANE), F32), ((nh, tm, MLA_QK_PAD), BF16), ((nh, tm // tk, FLASH_VT_ROWS, tk), BF16)]
    k, vt = pl.pallas_call(
        functools.partial(_kvproj_kernel, tk=tk),
        grid=(t // tm,),
        in_specs=[pl.BlockSpec((tm, MLA_KV_RANK), lambda i: (i, SMALL_CKV // MLA_KV_RANK)),
                  pl.BlockSpec((tm, LANE), lambda i: (i, SMALL_KROPE // LANE)),
                  pl.BlockSpec((8, MLA_KV_RANK), lambda i: (0, 0)),
                  pl.BlockSpec((MLA_KV_RANK, kw), lambda i: (0, 0)),
                  pl.BlockSpec((vw, MLA_KV_RANK), lambda i: (0, 0)),
                  pl.BlockSpec((tm, 3 * LANE), lambda i: (i % nsb, 0))],
        out_specs=[pl.BlockSpec((1, nh, tm, MLA_QK_PAD), lambda i: (i // nsb, 0, i % nsb, 0)),
                   pl.BlockSpec((1, nh, tm // tk, FLASH_VT_ROWS, tk), lambda i: (i // nsb, 0, i % nsb, 0, 0))],
        out_shape=[jax.ShapeDtypeStruct((batch, nh, seq, MLA_QK_PAD), BF16),
                   jax.ShapeDtypeStruct((batch, nh, seq // tk, FLASH_VT_ROWS, tk), BF16)],
        compiler_params=_params(("arbitrary",), kv_blocks, temp_bytes=2 * _nbytes((tm, kw), F32) + 4 * _nbytes((tm, MLA_KV_RANK), F32)),
        name="mla_kv_proj",
    )(small, small, kv_nw, wk, wv_t, rope_tab)
    return qt, k, vt


def _flash_kernel(qt_ref, k_ref, vt_ref, o_ref, s0_ref, s1_ref, p0_ref, p1_ref, acc_ref):
    tq = qt_ref.shape[3]
    n_kv, _, tk = vt_ref.shape[2:]

    def qk(j, s_ref):
        s = _dot(k_ref[0, 0, pl.ds(pl.multiple_of(j * tk, tk), tk), :], qt_ref[0, 0])
        s_ref[...] = s
        return jnp.max(s, axis=0, keepdims=True)

    def soft(m, cm, s_ref, p_ref):
        m_new = jnp.maximum(m, cm)
        p_ref[...] = jnp.exp2(s_ref[...] - m_new).astype(p_ref.dtype)
        return m_new, jnp.exp2(m - m_new)

    def pv(j, alpha, p_ref):
        acc_ref[...] = alpha * acc_ref[...] + _dot(vt_ref[0, 0, j], p_ref[...])

    def step(j, m, alpha_prev, cm, s_cur, s_nxt, p_prev, p_cur):
        cm_next = qk(j + 1, s_nxt)
        pv(j - 1, alpha_prev, p_prev)
        m, alpha = soft(m, cm, s_cur, p_cur)
        return m, alpha, cm_next

    acc_ref[...] = jnp.zeros(acc_ref.shape, acc_ref.dtype)
    m = jnp.full((1, tq), -jnp.inf, F32)
    m, alpha = soft(m, qk(0, s0_ref), s0_ref, p0_ref)
    cm = qk(1, s1_ref)

    def body(i, carry):
        m, alpha, cm = carry
        j = 2 * i + 1
        m, alpha, cm = step(j, m, alpha, cm, s1_ref, s0_ref, p0_ref, p1_ref)
        m, alpha, cm = step(j + 1, m, alpha, cm, s0_ref, s1_ref, p1_ref, p0_ref)
        return m, alpha, cm

    m, alpha, cm = lax.fori_loop(0, (n_kv - 2) // 2, body, (m, alpha, cm), unroll=FLASH_UNROLL)
    pv(n_kv - 2, alpha, p0_ref)
    m, alpha = soft(m, cm, s1_ref, p1_ref)
    pv(n_kv - 1, alpha, p1_ref)
    o_ref[...] = (acc_ref[0:MLA_V_DIM, :] / acc_ref[MLA_V_DIM:MLA_V_DIM + 1, :]).T.astype(o_ref.dtype)


def _flash_attention(qt, k, vt):
    batch, nh, seq, _ = k.shape
    n_kv, vt_rows, tk = vt.shape[2:]
    assert n_kv >= 2 and n_kv % 2 == 0, n_kv
    tq = _tile(seq, FLASH_TQ)
    nq = seq // tq
    blocks = [((MLA_QK_PAD, tq), BF16), ((seq, MLA_QK_PAD), BF16), ((n_kv, vt_rows, tk), BF16), ((tq, MLA_V_DIM), BF16)]
    scratch = [pltpu.VMEM((tk, tq), F32), pltpu.VMEM((tk, tq), F32), pltpu.VMEM((tk, tq), BF16), pltpu.VMEM((tk, tq), BF16),
               pltpu.VMEM((vt_rows, tq), F32)]
    scratch_bytes = 3 * _nbytes((tk, tq), F32) + _nbytes((vt_rows, tq), F32)
    return pl.pallas_call(
        _flash_kernel,
        grid=(batch, nh, nq),
        in_specs=[
            pl.BlockSpec((1, 1, MLA_QK_PAD, tq), lambda b, h, i: (b, h, 0, i)),
            pl.BlockSpec((1, 1, seq, MLA_QK_PAD), lambda b, h, i: (b, h, 0, 0)),
            pl.BlockSpec((1, 1, n_kv, vt_rows, tk), lambda b, h, i: (b, h, 0, 0, 0)),
        ],
        out_specs=pl.BlockSpec((tq, MLA_V_DIM), lambda b, h, i: (b * nq + i, h)),
        out_shape=jax.ShapeDtypeStruct((batch * seq, nh * MLA_V_DIM), BF16),
        scratch_shapes=scratch,
        compiler_params=_params(("arbitrary",) * 3, blocks, scratch_bytes, temp_bytes=4 * _nbytes((tk, tq), F32)),
        name="mla_flash_attention",
    )(qt, k, vt)


def _merge_kernel(a1_ref, w1_ref, a2_ref, w2_ref, g1_ref, g2_ref, o_ref):
    y1 = _dot(a1_ref[...], w1_ref[...])
    y2 = _dot(a2_ref[...], w2_ref[...])
    g1 = jax.nn.sigmoid(g1_ref[...].astype(F32))
    g2 = jax.nn.sigmoid(g2_ref[...].astype(F32))
    o_ref[...] = (g1 * y1 + g2 * y2).astype(o_ref.dtype)


def _merge(y_ssd_n, w_ssd_out, attn, w_mla_out, main):
    t = y_ssd_n.shape[0]
    tm = _tile(t, 512)
    tn = 512
    k1, k2 = SSD_D_INNER, MLA_N_HEADS * MLA_V_DIM
    blocks = [((tm, k1), BF16), ((k1, tn), BF16), ((tm, k2), BF16), ((k2, tn), BF16), ((tm, tn), BF16), ((tm, tn), BF16), ((tm, tn), BF16)]
    return pl.pallas_call(
        _merge_kernel,
        grid=(D_MODEL // tn, t // tm),
        in_specs=[
            pl.BlockSpec((tm, k1), lambda j, i: (i, 0)), pl.BlockSpec((k1, tn), lambda j, i: (0, j)),
            pl.BlockSpec((tm, k2), lambda j, i: (i, 0)), pl.BlockSpec((k2, tn), lambda j, i: (0, j)),
            pl.BlockSpec((tm, tn), lambda j, i: (i, MAIN_GSSD // tn + j)),
            pl.BlockSpec((tm, tn), lambda j, i: (i, MAIN_GMLA // tn + j)),
        ],
        out_specs=pl.BlockSpec((tm, tn), lambda j, i: (i, j)),
        out_shape=jax.ShapeDtypeStruct((t, D_MODEL), BF16),
        compiler_params=_params(("arbitrary",) * 2, blocks, temp_bytes=4 * _nbytes((tm, tn), F32)),
        name="branch_merge",
    )(y_ssd_n, w_ssd_out, attn, w_mla_out, main, main)


def _layer_norm_rows(r, prm_ref):
    mu = jnp.mean(r, axis=-1, keepdims=True)
    c = r - mu
    var = jnp.mean(c * c, axis=-1, keepdims=True)
    return c * lax.rsqrt(var + LN_EPS) * prm_ref[0:1, :] + prm_ref[1:2, :]


def _out_ln_kernel(a_ref, w_ref, h_ref, prm_ref, of_ref, ob_ref):
    y = _layer_norm_rows(DEEPNORM_ALPHA * h_ref[...] + _dot(a_ref[...], w_ref[...]), prm_ref)
    of_ref[...] = y
    ob_ref[...] = y.astype(ob_ref.dtype)


def _out_ln(merged, w_out, h, ln_prm):
    t = h.shape[0]
    tm = _tile(t, 256)
    d = D_MODEL
    blocks = [((tm, d), BF16), ((d, d), BF16), ((tm, d), F32), ((8, d), F32), ((tm, d), F32), ((tm, d), BF16)]
    row = lambda i: (i, 0)
    return pl.pallas_call(
        _out_ln_kernel,
        grid=(t // tm,),
        in_specs=[pl.BlockSpec((tm, d), row), pl.BlockSpec((d, d), lambda i: (0, 0)), pl.BlockSpec((tm, d), row),
                  pl.BlockSpec((8, d), lambda i: (0, 0))],
        out_specs=[pl.BlockSpec((tm, d), row), pl.BlockSpec((tm, d), row)],
        out_shape=[jax.ShapeDtypeStruct((t, d), F32), jax.ShapeDtypeStruct((t, d), BF16)],
        compiler_params=_params(("arbitrary",), blocks, temp_bytes=3 * _nbytes((tm, d), F32)),
        name="mixer_out_ln",
    )(merged, w_out, h, ln_prm)


def _ffn_up_kernel(a_ref, wg_ref, wu_ref, o_ref):
    a = a_ref[...]
    gate = _dot(a, wg_ref[...])
    o_ref[...] = (gate * jax.nn.sigmoid(gate) * _dot(a, wu_ref[...])).astype(o_ref.dtype)


def _ffn_up(hb, wg, wu):
    t = hb.shape[0]
    tm = _tile(t, 1024)
    tn = 512
    d = D_MODEL
    blocks = [((tm, d), BF16), ((d, tn), BF16), ((d, tn), BF16), ((tm, tn), BF16)]
    return pl.pallas_call(
        _ffn_up_kernel,
        grid=(FFN_HIDDEN // tn, t // tm),
        in_specs=[pl.BlockSpec((tm, d), lambda j, i: (i, 0)), pl.BlockSpec((d, tn), lambda j, i: (0, j)),
                  pl.BlockSpec((d, tn), lambda j, i: (0, j))],
        out_specs=pl.BlockSpec((tm, tn), lambda j, i: (i, j)),
        out_shape=jax.ShapeDtypeStruct((t, FFN_HIDDEN), BF16),
        compiler_params=_params(("arbitrary",) * 2, blocks, temp_bytes=3 * _nbytes((tm, tn), F32)),
        name="ffn_up",
    )(hb, wg, wu)


def _ffn_down_ln_kernel(a_ref, w_ref, h_ref, prm_ref, of_ref, ob_ref, acc_ref):
    kk = pl.program_id(1)

    @pl.when(kk == 0)
    def _init():
        acc_ref[...] = DEEPNORM_ALPHA * h_ref[...]

    acc_ref[...] += _dot(a_ref[...], w_ref[...])

    @pl.when(kk == pl.num_programs(1) - 1)
    def _finish():
        y = _layer_norm_rows(acc_ref[...], prm_ref)
        of_ref[...] = y
        ob_ref[...] = y.astype(ob_ref.dtype)


def _ffn_down_ln(act, w_down, h, ln_prm):
    t = h.shape[0]
    tm = _tile(t, 512)
    tk = 512
    d = D_MODEL
    blocks = [((tm, tk), BF16), ((tk, d), BF16), ((tm, d), F32), ((8, d), F32), ((tm, d), F32), ((tm, d), BF16)]
    row = lambda i, kk: (i, 0)
    return pl.pallas_call(
        _ffn_down_ln_kernel,
        grid=(t // tm, FFN_HIDDEN // tk),
        in_specs=[pl.BlockSpec((tm, tk), lambda i, kk: (i, kk)), pl.BlockSpec((tk, d), lambda i, kk: (kk, 0)),
                  pl.BlockSpec((tm, d), row), pl.BlockSpec((8, d), lambda i, kk: (0, 0))],
        out_specs=[pl.BlockSpec((tm, d), row), pl.BlockSpec((tm, d), row)],
        out_shape=[jax.ShapeDtypeStruct((t, d), F32), jax.ShapeDtypeStruct((t, d), BF16)],
        scratch_shapes=[pltpu.VMEM((tm, d), F32)],
        compiler_params=_params(("arbitrary",) * 2, blocks, _nbytes((tm, d), F32), temp_bytes=2 * _nbytes((tm, d), F32)),
        name="ffn_down_ln",
    )(act, w_down, h, ln_prm)


def _rows8(*rows):
    n = rows[0].shape[0]
    pad = jnp.zeros((8 - len(rows), n), F32)
    return jnp.concatenate([jnp.stack([r.astype(F32) for r in rows]), pad], axis=0)


def _rope_table(seq):
    pos = jnp.arange(seq, dtype=F32)
    inv_freq = ROPE_THETA ** (-jnp.arange(0, MLA_ROPE_DIM, 2, dtype=F32) / MLA_ROPE_DIM)
    ang = pos[:, None] * inv_freq[None, :]
    cos, sin = jnp.cos(ang), jnp.sin(ang)
    half = MLA_ROPE_DIM // 2
    z = lambda n: jnp.zeros((seq, n), F32)
    lanes = jnp.concatenate([cos, cos, z(LANE - 2 * half), -sin, z(LANE - half), z(half), sin, z(LANE - 2 * half)], axis=1)
    return lanes, jnp.concatenate([cos.T, sin.T], axis=0)


def _q_weight_t(w_uq):
    r = w_uq.shape[0]
    w = w_uq.reshape(r, MLA_N_HEADS, MLA_QK_DIM)
    w = jnp.pad(w, ((0, 0), (0, 0), (0, MLA_QK_PAD - MLA_QK_DIM)))
    return w.reshape(r, MLA_N_HEADS * MLA_QK_PAD).T.astype(BF16)


def _kv_weights(w_ukv):
    r = w_ukv.shape[0]
    w = w_ukv.reshape(r, MLA_N_HEADS, MLA_NOPE_DIM + MLA_V_DIM)
    wk = w[:, :, :MLA_NOPE_DIM].reshape(r, MLA_N_HEADS * MLA_NOPE_DIM)
    wv = w[:, :, MLA_NOPE_DIM:].reshape(r, MLA_N_HEADS * MLA_V_DIM)
    return wk.astype(BF16), wv.T.astype(BF16)


def kernel(x, w_in, conv_w, conv_b, ssd_a_log, ssd_dt_bias, ssd_d, ssd_norm_w, w_ssd_out, mla_q_norm_w, w_uq,
           mla_kv_norm_w, w_ukv, w_mla_out, w_out, ln1_g, ln1_b, w_ffn_gate, w_ffn_up, w_ffn_down, ln2_g, ln2_b):
    batch, seq, d = x.shape
    assert d == D_MODEL and seq % SSD_CHUNK == 0
    t = batch * seq
    hf = x.reshape(t, d)
    hb = hf.astype(BF16)
    rope_tab, rope_tab_t = _rope_table(seq)
    dt0 = MAIN_GSSD + 2 * SSD_N_HEADS
    q0 = dt0
    kv0 = q0 + MLA_Q_RANK
    kr0 = kv0 + MLA_KV_RANK
    g0 = kr0 + MLA_ROPE_DIM
    for l in range(w_in.shape[0]):
        wi = w_in[l]
        w_main = jnp.concatenate([wi[:, :MAIN_GSSD], wi[:, g0:]], axis=1).astype(BF16)
        w_small = jnp.concatenate(
            [wi[:, q0:kv0], wi[:, kv0:kr0], wi[:, MAIN_GSSD:dt0], wi[:, kr0:g0], jnp.zeros((d, LANE - MLA_ROPE_DIM), wi.dtype)],
            axis=1).astype(BF16)
        main = _matmul(hb, w_main, BF16, 1024, "in_proj_main")
        small = _matmul(hb, w_small, F32, SMALL_WIDTH, "in_proj_small")

        act = _conv_silu(main, _rows8(*[conv_w[l, i] for i in range(SSD_CONV_WIDTH)], conv_b[l]), batch, seq)
        yf, yb = _ssd_scan(act, small, _rows8(ssd_dt_bias[l].reshape(-1), ssd_a_log[l].reshape(-1)), batch, seq)
        y_ssd_n = _ssd_finish(yf, yb, act, main, _rows8(jnp.repeat(ssd_d[l], SSD_HEAD_DIM), ssd_norm_w[l]))

        wk, wv_t = _kv_weights(w_ukv[l])
        q_nw_lanes = jnp.broadcast_to(mla_q_norm_w[l].astype(F32)[:, None], (MLA_Q_RANK, LANE))
        qt, k, vt = _mla_proj(small, q_nw_lanes, _q_weight_t(w_uq[l]), _rows8(mla_kv_norm_w[l]), wk, wv_t,
                              rope_tab, rope_tab_t, batch, seq)
        attn = _flash_attention(qt, k, vt)

        merged = _merge(y_ssd_n, w_ssd_out[l].astype(BF16), attn, w_mla_out[l].astype(BF16), main)
        hf, hb = _out_ln(merged, w_out[l].astype(BF16), hf, _rows8(ln1_g[l], ln1_b[l]))
        ffn_act = _ffn_up(hb, w_ffn_gate[l].astype(BF16), w_ffn_up[l].astype(BF16))
        hf, hb = _ffn_down_ln(ffn_act, w_ffn_down[l].astype(BF16), hf, _rows8(ln2_g[l], ln2_b[l]))
    return hf.reshape(batch, seq, d)
```

```python
import functools

import jax
import jax.numpy as jnp
from jax import lax
from jax.experimental import pallas as pl
from jax.experimental.pallas import tpu as pltpu

F32 = jnp.float32
BF16 = jnp.bfloat16

D_MODEL = 2048
DEPTH = 4
SSD_D_INNER = 4096
SSD_HEAD_DIM = 64
SSD_N_HEADS = 64
SSD_N_GROUPS = 8
SSD_HEADS_PER_GROUP = 8
SSD_D_STATE = 128
SSD_CONV_WIDTH = 5
SSD_CHUNK = 128
SSD_GROUP_WIDTH = SSD_HEADS_PER_GROUP * SSD_HEAD_DIM
SSD_BC_WIDTH = SSD_N_GROUPS * SSD_D_STATE
SSD_CONV_DIM = SSD_D_INNER + 2 * SSD_BC_WIDTH
MLA_N_HEADS = 16
MLA_Q_RANK = 512
MLA_KV_RANK = 512
MLA_NOPE_DIM = 128
MLA_ROPE_DIM = 64
MLA_V_DIM = 128
MLA_QK_DIM = MLA_NOPE_DIM + MLA_ROPE_DIM
MLA_QK_PAD = 256
ROPE_THETA = 10000.0
FLASH_TQ = 512
FLASH_TK = 256
FLASH_VT_ROWS = MLA_V_DIM + 16
FLASH_UNROLL = 2
LOG2_E = 1.4426950408889634
FFN_HIDDEN = 5632
DEEPNORM_ALPHA = (2 * DEPTH) ** 0.25
RMS_EPS = 1e-6
LN_EPS = 1e-5

MAIN_Z = 0
MAIN_XBC = SSD_D_INNER
MAIN_GSSD = MAIN_XBC + SSD_CONV_DIM
MAIN_GMLA = MAIN_GSSD + D_MODEL
MAIN_WIDTH = MAIN_GMLA + D_MODEL
SMALL_Q = 0
SMALL_CKV = MLA_Q_RANK
SMALL_DT = SMALL_CKV + MLA_KV_RANK
SMALL_KROPE = SMALL_DT + 2 * SSD_N_HEADS
SMALL_WIDTH = SMALL_KROPE + 128
ACT_B = SSD_D_INNER
ACT_C = SSD_D_INNER + SSD_BC_WIDTH

V7X_VMEM_BYTES = 64 * 1024 * 1024
VMEM_LIMIT_CAP = 56 * 1024 * 1024
VMEM_LIMIT_FLOOR = 32 * 1024 * 1024
LANE = 128
BF16_SUBLANE_TILE = 16


def _nbytes(shape, dtype):
    n = 1
    for s in shape:
        n *= s
    return n * jnp.dtype(dtype).itemsize


def _params(semantics, blocks, scratch_bytes=0, temp_bytes=0):
    need = 2 * sum(_nbytes(s, d) for s, d in blocks) + scratch_bytes + temp_bytes
    limit = min(max(need + need // 4, VMEM_LIMIT_FLOOR), VMEM_LIMIT_CAP)
    return pltpu.CompilerParams(dimension_semantics=semantics, vmem_limit_bytes=limit)


def _tile(n, pref):
    t = min(n, pref)
    assert n % t == 0, (n, pref)
    return t


def _dot(a, b):
    return jnp.dot(a, b, preferred_element_type=F32)


def _dot_nt(a, b):
    return lax.dot_general(a, b, (((1,), (1,)), ((), ())), preferred_element_type=F32)


def _mm_kernel(a_ref, w_ref, o_ref):
    o_ref[...] = _dot(a_ref[...], w_ref[...]).astype(o_ref.dtype)


def _matmul(a, w, out_dtype, tn, name):
    m, k = a.shape
    n = w.shape[1]
    tm = _tile(m, 1024)
    tn = _tile(n, tn)
    blocks = [((tm, k), a.dtype), ((k, tn), w.dtype), ((tm, tn), out_dtype)]
    return pl.pallas_call(
        _mm_kernel,
        grid=(n // tn, m // tm),
        in_specs=[pl.BlockSpec((tm, k), lambda j, i: (i, 0)), pl.BlockSpec((k, tn), lambda j, i: (0, j))],
        out_specs=pl.BlockSpec((tm, tn), lambda j, i: (i, j)),
        out_shape=jax.ShapeDtypeStruct((m, n), out_dtype),
        compiler_params=_params(("arbitrary", "arbitrary"), blocks, temp_bytes=_nbytes((tm, tn), F32)),
        name=name,
    )(a, w)


def _conv_kernel(prev_ref, cur_ref, next_ref, w_ref, o_ref, *, n_seq_blocks):
    j = pl.program_id(1)
    cur = cur_ref[...].astype(F32)
    ts = cur.shape[0]
    has_prev = j > 0
    has_next = j < n_seq_blocks - 1
    halo = BF16_SUBLANE_TILE
    p0 = jnp.where(has_prev, prev_ref[halo - 2:halo - 1, :].astype(F32), 0.0)
    p1 = jnp.where(has_prev, prev_ref[halo - 1:halo, :].astype(F32), 0.0)
    n0 = jnp.where(has_next, next_ref[0:1, :].astype(F32), 0.0)
    n1 = jnp.where(has_next, next_ref[1:2, :].astype(F32), 0.0)
    row = lax.broadcasted_iota(jnp.int32, cur.shape, 0)
    xm1 = jnp.where(row == 0, p1, pltpu.roll(cur, 1, 0))
    xm2 = jnp.where(row == 0, p0, jnp.where(row == 1, p1, pltpu.roll(cur, 2, 0)))
    xp1 = jnp.where(row == ts - 1, n0, pltpu.roll(cur, ts - 1, 0))
    xp2 = jnp.where(row == ts - 2, n0, jnp.where(row == ts - 1, n1, pltpu.roll(cur, ts - 2, 0)))
    w = w_ref[...]
    y = xm2 * w[0:1] + xm1 * w[1:2] + cur * w[2:3] + xp1 * w[3:4] + xp2 * w[4:5] + w[5:6]
    o_ref[...] = (y * jax.nn.sigmoid(y)).astype(o_ref.dtype)


def _conv_silu(main, conv_prm, batch, seq):
    t = main.shape[0]
    ts = _tile(seq, 512)
    tc = 1024
    nsb = seq // ts
    col0 = MAIN_XBC // tc
    halo = BF16_SUBLANE_TILE
    rpb = ts // halo
    last_halo = t // halo - 1
    blocks = [((halo, tc), BF16), ((ts, tc), BF16), ((halo, tc), BF16), ((8, tc), F32), ((ts, tc), BF16)]
    return pl.pallas_call(
        functools.partial(_conv_kernel, n_seq_blocks=nsb),
        grid=(batch, nsb, SSD_CONV_DIM // tc),
        in_specs=[
            pl.BlockSpec((halo, tc), lambda b, j, c: (jnp.maximum((b * nsb + j) * rpb - 1, 0), col0 + c)),
            pl.BlockSpec((ts, tc), lambda b, j, c: (b * nsb + j, col0 + c)),
            pl.BlockSpec((halo, tc), lambda b, j, c: (jnp.minimum((b * nsb + j + 1) * rpb, last_halo), col0 + c)),
            pl.BlockSpec((8, tc), lambda b, j, c: (0, c)),
        ],
        out_specs=pl.BlockSpec((ts, tc), lambda b, j, c: (b * nsb + j, c)),
        out_shape=jax.ShapeDtypeStruct((t, SSD_CONV_DIM), BF16),
        compiler_params=_params(("arbitrary",) * 3, blocks, temp_bytes=8 * _nbytes((ts, tc), F32)),
        name="conv_silu",
    )(main, main, main, conv_prm)


def _ssd_kernel(xf_ref, bf_ref, cf_ref, dtf_ref, xb_ref, bb_ref, cb_ref, dtb_ref, prm_ref,
                yf_ref, yb_ref, st_ref, col_ref, row_ref, *, groups_per_step, expand_inter):
    q = SSD_CHUNK
    k = pl.program_id(1)
    gi = pl.program_id(2)
    half = SSD_N_HEADS

    @pl.when(jnp.logical_and(k == 0, gi == 0))
    def _zero_state():
        st_ref[...] = jnp.zeros(st_ref.shape, st_ref.dtype)

    @pl.when(gi == 0)
    def _per_chunk_decay_terms():
        lane = lax.broadcasted_iota(jnp.int32, (q, 2 * half), 1)
        is_fwd = lane < half
        x = jnp.where(is_fwd, dtf_ref[...], dtb_ref[...]) + prm_ref[0:1, :]
        dt = jnp.maximum(x, 0.0) + jnp.log1p(jnp.exp(-jnp.abs(x)))
        a = -jnp.exp(prm_ref[1:2, :]) * dt
        r = lax.broadcasted_iota(jnp.int32, (q, q), 0)
        c = lax.broadcasted_iota(jnp.int32, (q, q), 1)
        tril = (r >= c).astype(BF16)
        triu = (r <= c).astype(BF16)
        a_hi = a.astype(BF16)
        a_lo = (a - a_hi.astype(F32)).astype(BF16)
        p = jnp.where(is_fwd, _dot(tril, a_hi) + _dot(tril, a_lo), _dot(triu, a_hi) + _dot(triu, a_lo))
        p_tot = jnp.where(is_fwd[0:1], p[q - 1:q, :], p[0:1, :])
        col_ref[0] = p
        col_ref[1] = jnp.exp(p)
        row_ref[0] = p.T
        row_ref[1] = dt.T
        row_ref[2] = (dt * jnp.exp(p_tot - p)).T
        row_ref[3] = jnp.broadcast_to(jnp.exp(p_tot), (q, 2 * half)).T

    ri = lax.broadcasted_iota(jnp.int32, (q, q), 0)
    ci = lax.broadcasted_iota(jnp.int32, (q, q), 1)
    masks = (ri >= ci, ri <= ci)
    low_lanes = ci < SSD_HEAD_DIM
    dirs = ((xf_ref, bf_ref, cf_ref, yf_ref), (xb_ref, bb_ref, cb_ref, yb_ref))
    hd = SSD_HEAD_DIM
    zero_b = jnp.zeros((q, 2 * hd), BF16)
    if expand_inter:
        er = lax.broadcasted_iota(jnp.int32, (2 * half, SSD_GROUP_WIDTH), 0)
        ec = lax.broadcasted_iota(jnp.int32, (2 * half, SSD_GROUP_WIDTH), 1)
        expand = tuple((er == d * half + ec // hd).astype(BF16) for d in range(2))
    for gg in range(groups_per_step):
        g = gi * groups_per_step + gg
        shift = lax.rem(2 * half - g * SSD_HEADS_PER_GROUP, 2 * half)
        p_cols = pltpu.roll(col_ref[0], shift, 1)
        ep_cols = pltpu.roll(col_ref[1], shift, 1)
        for d, (x_ref, b_ref, c_ref, y_ref) in enumerate(dirs):
            bg = b_ref[:, gg * SSD_D_STATE:(gg + 1) * SSD_D_STATE]
            cg = c_ref[:, gg * SSD_D_STATE:(gg + 1) * SSD_D_STATE]
            cb_masked = jnp.where(masks[d], _dot_nt(cg, bg), 0.0)
            cg32 = cg.astype(F32)
            bt = bg.astype(F32).T
            r0 = pl.multiple_of(d * half + g * SSD_HEADS_PER_GROUP, SSD_HEADS_PER_GROUP)
            p_rows = row_ref[0, pl.ds(r0, SSD_HEADS_PER_GROUP), :]
            dt_rows = row_ref[1, pl.ds(r0, SSD_HEADS_PER_GROUP), :]
            w_rows = row_ref[2, pl.ds(r0, SSD_HEADS_PER_GROUP), :]
            dec_rows = row_ref[3, pl.ds(r0, SSD_HEADS_PER_GROUP), :]
            if expand_inter:
                z_all = _dot(cg, st_ref[d, g].astype(BF16))
                ep_hi = ep_cols.astype(BF16)
                ep_lo = (ep_cols - ep_hi.astype(F32)).astype(BF16)
                e_all = _dot(ep_hi, expand[d]) + _dot(ep_lo, expand[d])
                inter = e_all * z_all

            def head_terms(hh):
                li = d * half + hh
                seg = p_cols[:, li:li + 1] - p_rows[hh:hh + 1, :]
                l_mat = jnp.exp(jnp.minimum(seg, 0.0)) * (cb_masked * dt_rows[hh:hh + 1, :])
                btw = bt * w_rows[hh:hh + 1, :]
                c_scaled = None if expand_inter else (cg32 * ep_cols[:, li:li + 1]).astype(BF16)
                return l_mat.astype(BF16), c_scaled, btw.astype(BF16)

            for hp in range(SSD_HEADS_PER_GROUP // 2):
                h0, h1 = 2 * hp, 2 * hp + 1
                lo = (gg * SSD_HEADS_PER_GROUP + h0) * hd
                xp = x_ref[:, lo:lo + 2 * hd]
                x0 = jnp.where(low_lanes, xp, zero_b)
                x1 = jnp.where(low_lanes, zero_b, xp)
                sp = st_ref[d, g, :, h0 * hd:(h0 + 2) * hd]
                spb = None if expand_inter else sp.astype(BF16)
                l0, c0, w0 = head_terms(h0)
                l1, c1, w1 = head_terms(h1)
                x_rhs = jnp.concatenate([x0, x1], axis=0)
                if expand_inter:
                    y_pair = _dot(jnp.concatenate([l0, l1], axis=1), x_rhs) + inter[:, h0 * hd:(h0 + 2) * hd]
                else:
                    lhs = jnp.concatenate([l0, l1, c0, c1], axis=1)
                    rhs = jnp.concatenate([x_rhs, jnp.where(low_lanes, spb, zero_b), jnp.where(low_lanes, zero_b, spb)], axis=0)
                    y_pair = _dot(lhs, rhs)
                y_ref[:, lo:lo + 2 * hd] = y_pair.astype(y_ref.dtype)
                dec = jnp.where(low_lanes[0:1], dec_rows[h0:h0 + 1, :], dec_rows[h1:h1 + 1, :])
                st_ref[d, g, :, h0 * hd:(h0 + 2) * hd] = sp * dec + _dot(jnp.concatenate([w0, w1], axis=1), x_rhs)


def _ssd_scan(act, small, ssd_prm, batch, seq, gps=2, expand_inter=False):
    t = act.shape[0]
    q = SSD_CHUNK
    nc = seq // q
    xw = gps * SSD_GROUP_WIDTH
    bw = gps * SSD_D_STATE
    b0 = ACT_B // bw
    c0 = ACT_C // bw
    dtc = SMALL_DT // LANE

    def fwd(col0):
        return lambda b, k, gi: (b * nc + k, col0 + gi)

    def bwd(col0):
        return lambda b, k, gi: (b * nc + nc - 1 - k, col0 + gi)

    blocks = 2 * [((q, xw), BF16), ((q, bw), BF16), ((q, bw), BF16), ((q, LANE), F32), ((q, xw), BF16)] + [((8, LANE), F32)]
    scratch = [
        pltpu.VMEM((2, SSD_N_GROUPS, SSD_D_STATE, SSD_GROUP_WIDTH), F32),
        pltpu.VMEM((2, q, LANE), F32),
        pltpu.VMEM((4, LANE, q), F32),
    ]
    scratch_bytes = _nbytes((2, SSD_N_GROUPS, SSD_D_STATE, SSD_GROUP_WIDTH), F32) + 6 * _nbytes((q, LANE), F32)
    return pl.pallas_call(
        functools.partial(_ssd_kernel, groups_per_step=gps, expand_inter=expand_inter),
        grid=(batch, nc, SSD_N_GROUPS // gps),
        in_specs=[
            pl.BlockSpec((q, xw), fwd(0)), pl.BlockSpec((q, bw), fwd(b0)), pl.BlockSpec((q, bw), fwd(c0)),
            pl.BlockSpec((q, LANE), lambda b, k, gi: (b * nc + k, dtc)),
            pl.BlockSpec((q, xw), bwd(0)), pl.BlockSpec((q, bw), bwd(b0)), pl.BlockSpec((q, bw), bwd(c0)),
            pl.BlockSpec((q, LANE), lambda b, k, gi: (b * nc + nc - 1 - k, dtc)),
            pl.BlockSpec((8, LANE), lambda b, k, gi: (0, 0)),
        ],
        out_specs=[pl.BlockSpec((q, xw), fwd(0)), pl.BlockSpec((q, xw), bwd(0))],
        out_shape=[jax.ShapeDtypeStruct((t, SSD_D_INNER), BF16)] * 2,
        scratch_shapes=scratch,
        compiler_params=_params(("arbitrary",) * 3, blocks, scratch_bytes, temp_bytes=64 * _nbytes((q, q), F32)),
        name="ssd_scan",
    )(act, act, act, small, act, act, act, small, ssd_prm)


def _ssd_finish_kernel(yf_ref, yb_ref, xs_ref, z_ref, prm_ref, o_ref):
    z = z_ref[...].astype(F32)
    y = yf_ref[...].astype(F32) + yb_ref[...].astype(F32) + prm_ref[0:1, :] * xs_ref[...].astype(F32)
    y = y * (z * jax.nn.sigmoid(z))
    gw = SSD_GROUP_WIDTH
    for g in range(SSD_N_GROUPS):
        yg = y[:, g * gw:(g + 1) * gw]
        ms = jnp.mean(yg * yg, axis=-1, keepdims=True)
        o_ref[:, g * gw:(g + 1) * gw] = (yg * lax.rsqrt(ms + RMS_EPS) * prm_ref[1:2, g * gw:(g + 1) * gw]).astype(o_ref.dtype)


def _ssd_finish(yf, yb, act, main, fin_prm):
    t = yf.shape[0]
    ts = _tile(t, 256)
    di = SSD_D_INNER
    row_spec = pl.BlockSpec((ts, di), lambda i: (i, 0))
    blocks = 5 * [((ts, di), BF16)] + [((8, di), F32)]
    return pl.pallas_call(
        _ssd_finish_kernel,
        grid=(t // ts,),
        in_specs=[row_spec, row_spec, row_spec, row_spec, pl.BlockSpec((8, di), lambda i: (0, 0))],
        out_specs=row_spec,
        out_shape=jax.ShapeDtypeStruct((t, di), BF16),
        compiler_params=_params(("arbitrary",), blocks, temp_bytes=4 * _nbytes((ts, di), F32)),
        name="ssd_finish",
    )(yf, yb, act, main, fin_prm)


def _rope_lanes(t, tab):
    return (t * tab[:, 0:LANE] + pltpu.roll(t, LANE - MLA_ROPE_DIM // 2, 1) * tab[:, LANE:2 * LANE]
            + pltpu.roll(t, MLA_ROPE_DIM // 2, 1) * tab[:, 2 * LANE:3 * LANE])


def _rms_rows(x, w):
    return x * lax.rsqrt(jnp.mean(x * x, axis=-1, keepdims=True) + RMS_EPS) * w


def _qproj_kernel(x_ref, nw_ref, w_ref, tab_ref, qt_ref, *, scale):
    xt = x_ref[...].T
    tm = xt.shape[1]
    inv = lax.rsqrt(jnp.mean(xt * xt, axis=0, keepdims=True) + RMS_EPS)
    nw = jnp.concatenate([nw_ref[...]] * (tm // LANE), axis=1)
    xnt = (xt * inv * nw).astype(BF16)
    half = MLA_ROPE_DIM // 2
    cos = tab_ref[0:half, :]
    sin = tab_ref[half:2 * half, :]
    r0 = MLA_NOPE_DIM
    for h in range(MLA_N_HEADS):
        qt = _dot(w_ref[h * MLA_QK_PAD:(h + 1) * MLA_QK_PAD, :], xnt)
        x1 = qt[r0:r0 + half]
        x2 = qt[r0 + half:r0 + 2 * half]
        qt_ref[0, h, 0:r0, :] = (qt[0:r0] * scale).astype(qt_ref.dtype)
        qt_ref[0, h, r0:r0 + half, :] = ((x1 * cos - x2 * sin) * scale).astype(qt_ref.dtype)
        qt_ref[0, h, r0 + half:r0 + 2 * half, :] = ((x2 * cos + x1 * sin) * scale).astype(qt_ref.dtype)
        qt_ref[0, h, r0 + 2 * half:MLA_QK_PAD, :] = jnp.zeros((MLA_QK_PAD - MLA_QK_DIM, tm), qt_ref.dtype)


def _kvproj_kernel(x_ref, kr_ref, nw_ref, wk_ref, wvt_ref, tab_ref, k_ref, vt_ref, *, tk):
    cn = _rms_rows(x_ref[...], nw_ref[0:1, :])
    tm = cn.shape[0]
    k_all = _dot(cn.astype(BF16), wk_ref[...])
    vt_all = _dot(wvt_ref[...], cn.T.astype(BF16))
    k_rope = _rope_lanes(kr_ref[...], tab_ref[...]).astype(k_ref.dtype)
    pad_rows = FLASH_VT_ROWS - MLA_V_DIM
    ones_rows = (lax.broadcasted_iota(jnp.int32, (pad_rows, tk), 0) == 0).astype(vt_ref.dtype)
    for h in range(MLA_N_HEADS):
        k_ref[0, h, :, 0:LANE] = k_all[:, h * MLA_NOPE_DIM:(h + 1) * MLA_NOPE_DIM].astype(k_ref.dtype)
        k_ref[0, h, :, LANE:2 * LANE] = k_rope
        for c in range(tm // tk):
            vt_ref[0, h, c, 0:MLA_V_DIM, :] = vt_all[h * MLA_V_DIM:(h + 1) * MLA_V_DIM, c * tk:(c + 1) * tk].astype(vt_ref.dtype)
            vt_ref[0, h, c, MLA_V_DIM:FLASH_VT_ROWS, :] = ones_rows


def _mla_proj(small, q_nw_lanes, wq_t, kv_nw, wk, wv_t, rope_tab, rope_tab_t, batch, seq, flash_tk=FLASH_TK):
    t = small.shape[0]
    tk = _tile(seq, flash_tk)
    tm = _tile(seq, max(512, tk))
    nsb = seq // tm
    nh = MLA_N_HEADS
    qw = nh * MLA_QK_PAD
    kw = nh * MLA_NOPE_DIM
    vw = nh * MLA_V_DIM
    q_blocks = [((tm, MLA_Q_RANK), F32), ((MLA_Q_RANK, LANE), F32), ((qw, MLA_Q_RANK), BF16), ((MLA_ROPE_DIM, tm), F32),
                ((nh, MLA_QK_PAD, tm), BF16)]
    qt = pl.pallas_call(
        functools.partial(_qproj_kernel, scale=MLA_QK_DIM ** -0.5 * LOG2_E),
        grid=(t // tm,),
        in_specs=[pl.BlockSpec((tm, MLA_Q_RANK), lambda i: (i, SMALL_Q // MLA_Q_RANK)),
                  pl.BlockSpec((MLA_Q_RANK, LANE), lambda i: (0, 0)),
                  pl.BlockSpec((qw, MLA_Q_RANK), lambda i: (0, 0)),
                  pl.BlockSpec((MLA_ROPE_DIM, tm), lambda i: (0, i % nsb))],
        out_specs=pl.BlockSpec((1, nh, MLA_QK_PAD, tm), lambda i: (i // nsb, 0, 0, i % nsb)),
        out_shape=jax.ShapeDtypeStruct((batch, nh, MLA_QK_PAD, seq), BF16),
        compiler_params=_params(("arbitrary",), q_blocks, temp_bytes=6 * _nbytes((tm, MLA_Q_RANK), F32)),
        name="mla_q_proj",
    )(small, q_nw_lanes, wq_t, rope_tab_t)
    kv_blocks = [((tm, MLA_KV_RANK), F32), ((tm, LANE), F32), ((8, MLA_KV_RANK), F32), ((MLA_KV_RANK, kw), BF16),
                 ((vw, MLA_KV_RANK), BF16), ((tm, 3 * LANE), F32), ((nh, tm, MLA_QK_PAD), BF16), ((nh, tm // tk, FLASH_VT_ROWS, tk), BF16)]
    k, vt = pl.pallas_call(
        functools.partial(_kvproj_kernel, tk=tk),
        grid=(t // tm,),
        in_specs=[pl.BlockSpec((tm, MLA_KV_RANK), lambda i: (i, SMALL_CKV // MLA_KV_RANK)),
                  pl.BlockSpec((tm, LANE), lambda i: (i, SMALL_KROPE // LANE)),
                  pl.BlockSpec((8, MLA_KV_RANK), lambda i: (0, 0)),
                  pl.BlockSpec((MLA_KV_RANK, kw), lambda i: (0, 0)),
                  pl.BlockSpec((vw, MLA_KV_RANK), lambda i: (0, 0)),
                  pl.BlockSpec((tm, 3 * LANE), lambda i: (i % nsb, 0))],
        out_specs=[pl.BlockSpec((1, nh, tm, MLA_QK_PAD), lambda i: (i // nsb, 0, i % nsb, 0)),
                   pl.BlockSpec((1, nh, tm // tk, FLASH_VT_ROWS, tk), lambda i: (i // nsb, 0, i % nsb, 0, 0))],
        out_shape=[jax.ShapeDtypeStruct((batch, nh, seq, MLA_QK_PAD), BF16),
                   jax.ShapeDtypeStruct((batch, nh, seq // tk, FLASH_VT_ROWS, tk), BF16)],
        compiler_params=_params(("arbitrary",), kv_blocks, temp_bytes=2 * _nbytes((tm, kw), F32) + 4 * _nbytes((tm, MLA_KV_RANK), F32)),
        name="mla_kv_proj",
    )(small, small, kv_nw, wk, wv_t, rope_tab)
    return qt, k, vt


def _flash_kernel(qt_ref, k_ref, vt_ref, o_ref, s0_ref, s1_ref, p0_ref, p1_ref, acc_ref, *, unroll):
    tq = qt_ref.shape[3]
    n_kv, _, tk = vt_ref.shape[2:]

    def qk(j, s_ref):
        s = _dot(k_ref[0, 0, pl.ds(pl.multiple_of(j * tk, tk), tk), :], qt_ref[0, 0])
        s_ref[...] = s
        return jnp.max(s, axis=0, keepdims=True)

    def soft(m, cm, s_ref, p_ref):
        m_new = jnp.maximum(m, cm)
        p_ref[...] = jnp.exp2(s_ref[...] - m_new).astype(p_ref.dtype)
        return m_new, jnp.exp2(m - m_new)

    def pv(j, alpha, p_ref):
        acc_ref[...] = alpha * acc_ref[...] + _dot(vt_ref[0, 0, j], p_ref[...])

    def step(j, m, alpha_prev, cm, s_cur, s_nxt, p_prev, p_cur):
        cm_next = qk(j + 1, s_nxt)
        pv(j - 1, alpha_prev, p_prev)
        m, alpha = soft(m, cm, s_cur, p_cur)
        return m, alpha, cm_next

    acc_ref[...] = jnp.zeros(acc_ref.shape, acc_ref.dtype)
    m = jnp.full((1, tq), -jnp.inf, F32)
    m, alpha = soft(m, qk(0, s0_ref), s0_ref, p0_ref)
    cm = qk(1, s1_ref)

    def body(i, carry):
        m, alpha, cm = carry
        j = 2 * i + 1
        m, alpha, cm = step(j, m, alpha, cm, s1_ref, s0_ref, p0_ref, p1_ref)
        m, alpha, cm = step(j + 1, m, alpha, cm, s0_ref, s1_ref, p1_ref, p0_ref)
        return m, alpha, cm

    m, alpha, cm = lax.fori_loop(0, (n_kv - 2) // 2, body, (m, alpha, cm), unroll=unroll)
    pv(n_kv - 2, alpha, p0_ref)
    m, alpha = soft(m, cm, s1_ref, p1_ref)
    pv(n_kv - 1, alpha, p1_ref)
    o_ref[...] = (acc_ref[0:MLA_V_DIM, :] / acc_ref[MLA_V_DIM:MLA_V_DIM + 1, :]).T.astype(o_ref.dtype)


def _flash_attention(qt, k, vt, flash_tq=FLASH_TQ, unroll=FLASH_UNROLL):
    batch, nh, seq, _ = k.shape
    n_kv, vt_rows, tk = vt.shape[2:]
    assert n_kv >= 2 and n_kv % 2 == 0, n_kv
    tq = _tile(seq, flash_tq)
    nq = seq // tq
    blocks = [((MLA_QK_PAD, tq), BF16), ((seq, MLA_QK_PAD), BF16), ((n_kv, vt_rows, tk), BF16), ((tq, MLA_V_DIM), BF16)]
    scratch = [pltpu.VMEM((tk, tq), F32), pltpu.VMEM((tk, tq), F32), pltpu.VMEM((tk, tq), BF16), pltpu.VMEM((tk, tq), BF16),
               pltpu.VMEM((vt_rows, tq), F32)]
    scratch_bytes = 3 * _nbytes((tk, tq), F32) + _nbytes((vt_rows, tq), F32)
    return pl.pallas_call(
        functools.partial(_flash_kernel, unroll=unroll),
        grid=(batch, nh, nq),
        in_specs=[
            pl.BlockSpec((1, 1, MLA_QK_PAD, tq), lambda b, h, i: (b, h, 0, i)),
            pl.BlockSpec((1, 1, seq, MLA_QK_PAD), lambda b, h, i: (b, h, 0, 0)),
            pl.BlockSpec((1, 1, n_kv, vt_rows, tk), lambda b, h, i: (b, h, 0, 0, 0)),
        ],
        out_specs=pl.BlockSpec((tq, MLA_V_DIM), lambda b, h, i: (b * nq + i, h)),
        out_shape=jax.ShapeDtypeStruct((batch * seq, nh * MLA_V_DIM), BF16),
        scratch_shapes=scratch,
        compiler_params=_params(("arbitrary",) * 3, blocks, scratch_bytes, temp_bytes=4 * _nbytes((tk, tq), F32)),
        name="mla_flash_attention",
    )(qt, k, vt)


def _merge_kernel(a1_ref, w1_ref, a2_ref, w2_ref, g1_ref, g2_ref, o_ref):
    y1 = _dot(a1_ref[...], w1_ref[...])
    y2 = _dot(a2_ref[...], w2_ref[...])
    g1 = jax.nn.sigmoid(g1_ref[...].astype(F32))
    g2 = jax.nn.sigmoid(g2_ref[...].astype(F32))
    o_ref[...] = (g1 * y1 + g2 * y2).astype(o_ref.dtype)


def _merge(y_ssd_n, w_ssd_out, attn, w_mla_out, main):
    t = y_ssd_n.shape[0]
    tm = _tile(t, 512)
    tn = 512
    k1, k2 = SSD_D_INNER, MLA_N_HEADS * MLA_V_DIM
    blocks = [((tm, k1), BF16), ((k1, tn), BF16), ((tm, k2), BF16), ((k2, tn), BF16), ((tm, tn), BF16), ((tm, tn), BF16), ((tm, tn), BF16)]
    return pl.pallas_call(
        _merge_kernel,
        grid=(D_MODEL // tn, t // tm),
        in_specs=[
            pl.BlockSpec((tm, k1), lambda j, i: (i, 0)), pl.BlockSpec((k1, tn), lambda j, i: (0, j)),
            pl.BlockSpec((tm, k2), lambda j, i: (i, 0)), pl.BlockSpec((k2, tn), lambda j, i: (0, j)),
            pl.BlockSpec((tm, tn), lambda j, i: (i, MAIN_GSSD // tn + j)),
            pl.BlockSpec((tm, tn), lambda j, i: (i, MAIN_GMLA // tn + j)),
        ],
        out_specs=pl.BlockSpec((tm, tn), lambda j, i: (i, j)),
        out_shape=jax.ShapeDtypeStruct((t, D_MODEL), BF16),
        compiler_params=_params(("arbitrary",) * 2, blocks, temp_bytes=4 * _nbytes((tm, tn), F32)),
        name="branch_merge",
    )(y_ssd_n, w_ssd_out, attn, w_mla_out, main, main)


def _layer_norm_rows(r, prm_ref):
    mu = jnp.mean(r, axis=-1, keepdims=True)
    c = r - mu
    var = jnp.mean(c * c, axis=-1, keepdims=True)
    return c * lax.rsqrt(var + LN_EPS) * prm_ref[0:1, :] + prm_ref[1:2, :]


def _out_ln_kernel(a_ref, w_ref, h_ref, prm_ref, of_ref, ob_ref):
    y = _layer_norm_rows(DEEPNORM_ALPHA * h_ref[...] + _dot(a_ref[...], w_ref[...]), prm_ref)
    of_ref[...] = y
    ob_ref[...] = y.astype(ob_ref.dtype)


def _out_ln(merged, w_out, h, ln_prm):
    t = h.shape[0]
    tm = _tile(t, 256)
    d = D_MODEL
    blocks = [((tm, d), BF16), ((d, d), BF16), ((tm, d), F32), ((8, d), F32), ((tm, d), F32), ((tm, d), BF16)]
    row = lambda i: (i, 0)
    return pl.pallas_call(
        _out_ln_kernel,
        grid=(t // tm,),
        in_specs=[pl.BlockSpec((tm, d), row), pl.BlockSpec((d, d), lambda i: (0, 0)), pl.BlockSpec((tm, d), row),
                  pl.BlockSpec((8, d), lambda i: (0, 0))],
        out_specs=[pl.BlockSpec((tm, d), row), pl.BlockSpec((tm, d), row)],
        out_shape=[jax.ShapeDtypeStruct((t, d), F32), jax.ShapeDtypeStruct((t, d), BF16)],
        compiler_params=_params(("arbitrary",), blocks, temp_bytes=3 * _nbytes((tm, d), F32)),
        name="mixer_out_ln",
    )(merged, w_out, h, ln_prm)


def _ffn_up_kernel(a_ref, wg_ref, wu_ref, o_ref):
    a = a_ref[...]
    gate = _dot(a, wg_ref[...])
    o_ref[...] = (gate * jax.nn.sigmoid(gate) * _dot(a, wu_ref[...])).astype(o_ref.dtype)


def _ffn_up(hb, wg, wu):
    t = hb.shape[0]
    tm = _tile(t, 1024)
    tn = 512
    d = D_MODEL
    blocks = [((tm, d), BF16), ((d, tn), BF16), ((d, tn), BF16), ((tm, tn), BF16)]
    return pl.pallas_call(
        _ffn_up_kernel,
        grid=(FFN_HIDDEN // tn, t // tm),
        in_specs=[pl.BlockSpec((tm, d), lambda j, i: (i, 0)), pl.BlockSpec((d, tn), lambda j, i: (0, j)),
                  pl.BlockSpec((d, tn), lambda j, i: (0, j))],
        out_specs=pl.BlockSpec((tm, tn), lambda j, i: (i, j)),
        out_shape=jax.ShapeDtypeStruct((t, FFN_HIDDEN), BF16),
        compiler_params=_params(("arbitrary",) * 2, blocks, temp_bytes=3 * _nbytes((tm, tn), F32)),
        name="ffn_up",
    )(hb, wg, wu)


def _ffn_down_ln_kernel(a_ref, w_ref, h_ref, prm_ref, of_ref, ob_ref, acc_ref):
    kk = pl.program_id(1)

    @pl.when(kk == 0)
    def _init():
        acc_ref[...] = DEEPNORM_ALPHA * h_ref[...]

    acc_ref[...] += _dot(a_ref[...], w_ref[...])

    @pl.when(kk == pl.num_programs(1) - 1)
    def _finish():
        y = _layer_norm_rows(acc_ref[...], prm_ref)
        of_ref[...] = y
        ob_ref[...] = y.astype(ob_ref.dtype)


def _ffn_down_ln(act, w_down, h, ln_prm, tm=512, tk=512):
    t = h.shape[0]
    tm = _tile(t, tm)
    d = D_MODEL
    blocks = [((tm, tk), BF16), ((tk, d), BF16), ((tm, d), F32), ((8, d), F32), ((tm, d), F32), ((tm, d), BF16)]
    row = lambda i, kk: (i, 0)
    return pl.pallas_call(
        _ffn_down_ln_kernel,
        grid=(t // tm, FFN_HIDDEN // tk),
        in_specs=[pl.BlockSpec((tm, tk), lambda i, kk: (i, kk)), pl.BlockSpec((tk, d), lambda i, kk: (kk, 0)),
                  pl.BlockSpec((tm, d), row), pl.BlockSpec((8, d), lambda i, kk: (0, 0))],
        out_specs=[pl.BlockSpec((tm, d), row), pl.BlockSpec((tm, d), row)],
        out_shape=[jax.ShapeDtypeStruct((t, d), F32), jax.ShapeDtypeStruct((t, d), BF16)],
        scratch_shapes=[pltpu.VMEM((tm, d), F32)],
        compiler_params=_params(("arbitrary",) * 2, blocks, _nbytes((tm, d), F32), temp_bytes=2 * _nbytes((tm, d), F32)),
        name="ffn_down_ln",
    )(act, w_down, h, ln_prm)


def _rows8(*rows):
    n = rows[0].shape[0]
    pad = jnp.zeros((8 - len(rows), n), F32)
    return jnp.concatenate([jnp.stack([r.astype(F32) for r in rows]), pad], axis=0)


def _rope_table(seq):
    pos = jnp.arange(seq, dtype=F32)
    inv_freq = ROPE_THETA ** (-jnp.arange(0, MLA_ROPE_DIM, 2, dtype=F32) / MLA_ROPE_DIM)
    ang = pos[:, None] * inv_freq[None, :]
    cos, sin = jnp.cos(ang), jnp.sin(ang)
    half = MLA_ROPE_DIM // 2
    z = lambda n: jnp.zeros((seq, n), F32)
    lanes = jnp.concatenate([cos, cos, z(LANE - 2 * half), -sin, z(LANE - half), z(half), sin, z(LANE - 2 * half)], axis=1)
    return lanes, jnp.concatenate([cos.T, sin.T], axis=0)


def _q_weight_t(w_uq):
    r = w_uq.shape[0]
    w = w_uq.reshape(r, MLA_N_HEADS, MLA_QK_DIM)
    w = jnp.pad(w, ((0, 0), (0, 0), (0, MLA_QK_PAD - MLA_QK_DIM)))
    return w.reshape(r, MLA_N_HEADS * MLA_QK_PAD).T.astype(BF16)


def _kv_weights(w_ukv):
    r = w_ukv.shape[0]
    w = w_ukv.reshape(r, MLA_N_HEADS, MLA_NOPE_DIM + MLA_V_DIM)
    wk = w[:, :, :MLA_NOPE_DIM].reshape(r, MLA_N_HEADS * MLA_NOPE_DIM)
    wv = w[:, :, MLA_NOPE_DIM:].reshape(r, MLA_N_HEADS * MLA_V_DIM)
    return wk.astype(BF16), wv.T.astype(BF16)


LAYER_VARIANTS = (
    dict(ssd_gps=2, ssd_expand=False, flash_tq=512, flash_tk=256, flash_unroll=2, down_tm=512, down_tk=512),
    dict(ssd_gps=4, ssd_expand=False, flash_tq=512, flash_tk=256, flash_unroll=4, down_tm=512, down_tk=1408),
    dict(ssd_gps=2, ssd_expand=True, flash_tq=512, flash_tk=512, flash_unroll=2, down_tm=256, down_tk=2816),
    dict(ssd_gps=4, ssd_expand=True, flash_tq=1024, flash_tk=256, flash_unroll=2, down_tm=256, down_tk=1408),
)


def kernel(x, w_in, conv_w, conv_b, ssd_a_log, ssd_dt_bias, ssd_d, ssd_norm_w, w_ssd_out, mla_q_norm_w, w_uq,
           mla_kv_norm_w, w_ukv, w_mla_out, w_out, ln1_g, ln1_b, w_ffn_gate, w_ffn_up, w_ffn_down, ln2_g, ln2_b):
    batch, seq, d = x.shape
    assert d == D_MODEL and seq % SSD_CHUNK == 0
    t = batch * seq
    hf = x.reshape(t, d)
    hb = hf.astype(BF16)
    rope_tab, rope_tab_t = _rope_table(seq)
    dt0 = MAIN_GSSD + 2 * SSD_N_HEADS
    q0 = dt0
    kv0 = q0 + MLA_Q_RANK
    kr0 = kv0 + MLA_KV_RANK
    g0 = kr0 + MLA_ROPE_DIM
    for l in range(w_in.shape[0]):
        cfg = LAYER_VARIANTS[l % len(LAYER_VARIANTS)]
        wi = w_in[l]
        w_main = jnp.concatenate([wi[:, :MAIN_GSSD], wi[:, g0:]], axis=1).astype(BF16)
        w_small = jnp.concatenate(
            [wi[:, q0:kv0], wi[:, kv0:kr0], wi[:, MAIN_GSSD:dt0], wi[:, kr0:g0], jnp.zeros((d, LANE - MLA_ROPE_DIM), wi.dtype)],
            axis=1).astype(BF16)
        main = _matmul(hb, w_main, BF16, 1024, "in_proj_main")
        small = _matmul(hb, w_small, F32, SMALL_WIDTH, "in_proj_small")

        act = _conv_silu(main, _rows8(*[conv_w[l, i] for i in range(SSD_CONV_WIDTH)], conv_b[l]), batch, seq)
        yf, yb = _ssd_scan(act, small, _rows8(ssd_dt_bias[l].reshape(-1), ssd_a_log[l].reshape(-1)), batch, seq,
                           gps=cfg['ssd_gps'], expand_inter=cfg['ssd_expand'])
        y_ssd_n = _ssd_finish(yf, yb, act, main, _rows8(jnp.repeat(ssd_d[l], SSD_HEAD_DIM), ssd_norm_w[l]))

        wk, wv_t = _kv_weights(w_ukv[l])
        q_nw_lanes = jnp.broadcast_to(mla_q_norm_w[l].astype(F32)[:, None], (MLA_Q_RANK, LANE))
        qt, k, vt = _mla_proj(small, q_nw_lanes, _q_weight_t(w_uq[l]), _rows8(mla_kv_norm_w[l]), wk, wv_t,
                              rope_tab, rope_tab_t, batch, seq, flash_tk=cfg['flash_tk'])
        attn = _flash_attention(qt, k, vt, flash_tq=cfg['flash_tq'], unroll=cfg['flash_unroll'])

        merged = _merge(y_ssd_n, w_ssd_out[l].astype(BF16), attn, w_mla_out[l].astype(BF16), main)
        hf, hb = _out_ln(merged, w_out[l].astype(BF16), hf, _rows8(ln1_g[l], ln1_b[l]))
        ffn_act = _ffn_up(hb, w_ffn_gate[l].astype(BF16), w_ffn_up[l].astype(BF16))
        hf, hb = _ffn_down_ln(ffn_act, w_ffn_down[l].astype(BF16), hf, _rows8(ln2_g[l], ln2_b[l]), tm=cfg['down_tm'], tk=cfg['down_tk'])
    return hf.reshape(batch, seq, d)
```

```python
import functools

import jax
import jax.numpy as jnp
from jax import lax
from jax.experimental import pallas as pl
from jax.experimental.pallas import tpu as pltpu

F32 = jnp.float32
BF16 = jnp.bfloat16

D_MODEL = 2048
DEPTH = 4
SSD_D_INNER = 4096
SSD_HEAD_DIM = 64
SSD_N_HEADS = 64
SSD_N_GROUPS = 8
SSD_HEADS_PER_GROUP = 8
SSD_D_STATE = 128
SSD_CONV_WIDTH = 5
SSD_CHUNK = 128
SSD_GROUP_WIDTH = SSD_HEADS_PER_GROUP * SSD_HEAD_DIM
SSD_BC_WIDTH = SSD_N_GROUPS * SSD_D_STATE
SSD_CONV_DIM = SSD_D_INNER + 2 * SSD_BC_WIDTH
MLA_N_HEADS = 16
MLA_Q_RANK = 512
MLA_KV_RANK = 512
MLA_NOPE_DIM = 128
MLA_ROPE_DIM = 64
MLA_V_DIM = 128
MLA_QK_DIM = MLA_NOPE_DIM + MLA_ROPE_DIM
MLA_QK_PAD = 256
ROPE_THETA = 10000.0
FLASH_TQ = 512
FLASH_TK = 256
FLASH_VT_ROWS = MLA_V_DIM + 16
FLASH_UNROLL = 2
LOG2_E = 1.4426950408889634
FFN_HIDDEN = 5632
DEEPNORM_ALPHA = (2 * DEPTH) ** 0.25
RMS_EPS = 1e-6
LN_EPS = 1e-5

MAIN_Z = 0
MAIN_XBC = SSD_D_INNER
MAIN_GSSD = MAIN_XBC + SSD_CONV_DIM
MAIN_GMLA = MAIN_GSSD + D_MODEL
MAIN_WIDTH = MAIN_GMLA + D_MODEL
SMALL_Q = 0
SMALL_CKV = MLA_Q_RANK
SMALL_DT = SMALL_CKV + MLA_KV_RANK
SMALL_KROPE = SMALL_DT + 2 * SSD_N_HEADS
SMALL_WIDTH = SMALL_KROPE + 128
ACT_B = SSD_D_INNER
ACT_C = SSD_D_INNER + SSD_BC_WIDTH

V7X_VMEM_BYTES = 64 * 1024 * 1024
VMEM_LIMIT_CAP = 56 * 1024 * 1024
VMEM_LIMIT_FLOOR = 32 * 1024 * 1024
LANE = 128
BF16_SUBLANE_TILE = 16


def _nbytes(shape, dtype):
    n = 1
    for s in shape:
        n *= s
    return n * jnp.dtype(dtype).itemsize


def _params(semantics, blocks, scratch_bytes=0, temp_bytes=0):
    need = 2 * sum(_nbytes(s, d) for s, d in blocks) + scratch_bytes + temp_bytes
    limit = min(max(need + need // 4, VMEM_LIMIT_FLOOR), VMEM_LIMIT_CAP)
    return pltpu.CompilerParams(dimension_semantics=semantics, vmem_limit_bytes=limit)


def _tile(n, pref):
    t = min(n, pref)
    assert n % t == 0, (n, pref)
    return t


def _dot(a, b):
    return jnp.dot(a, b, preferred_element_type=F32)


def _dot_nt(a, b):
    return lax.dot_general(a, b, (((1,), (1,)), ((), ())), preferred_element_type=F32)


def _mm_kernel(a_ref, w_ref, o_ref):
    o_ref[...] = _dot(a_ref[...], w_ref[...]).astype(o_ref.dtype)


def _matmul(a, w, out_dtype, tn, name):
    m, k = a.shape
    n = w.shape[1]
    tm = _tile(m, 1024)
    tn = _tile(n, tn)
    blocks = [((tm, k), a.dtype), ((k, tn), w.dtype), ((tm, tn), out_dtype)]
    return pl.pallas_call(
        _mm_kernel,
        grid=(n // tn, m // tm),
        in_specs=[pl.BlockSpec((tm, k), lambda j, i: (i, 0)), pl.BlockSpec((k, tn), lambda j, i: (0, j))],
        out_specs=pl.BlockSpec((tm, tn), lambda j, i: (i, j)),
        out_shape=jax.ShapeDtypeStruct((m, n), out_dtype),
        compiler_params=_params(("arbitrary", "arbitrary"), blocks, temp_bytes=_nbytes((tm, tn), F32)),
        name=name,
    )(a, w)


def _conv_kernel(prev_ref, cur_ref, next_ref, w_ref, o_ref, *, n_seq_blocks):
    j = pl.program_id(1)
    cur = cur_ref[...].astype(F32)
    ts = cur.shape[0]
    has_prev = j > 0
    has_next = j < n_seq_blocks - 1
    halo = BF16_SUBLANE_TILE
    p0 = jnp.where(has_prev, prev_ref[halo - 2:halo - 1, :].astype(F32), 0.0)
    p1 = jnp.where(has_prev, prev_ref[halo - 1:halo, :].astype(F32), 0.0)
    n0 = jnp.where(has_next, next_ref[0:1, :].astype(F32), 0.0)
    n1 = jnp.where(has_next, next_ref[1:2, :].astype(F32), 0.0)
    row = lax.broadcasted_iota(jnp.int32, cur.shape, 0)
    xm1 = jnp.where(row == 0, p1, pltpu.roll(cur, 1, 0))
    xm2 = jnp.where(row == 0, p0, jnp.where(row == 1, p1, pltpu.roll(cur, 2, 0)))
    xp1 = jnp.where(row == ts - 1, n0, pltpu.roll(cur, ts - 1, 0))
    xp2 = jnp.where(row == ts - 2, n0, jnp.where(row == ts - 1, n1, pltpu.roll(cur, ts - 2, 0)))
    w = w_ref[...]
    y = xm2 * w[0:1] + xm1 * w[1:2] + cur * w[2:3] + xp1 * w[3:4] + xp2 * w[4:5] + w[5:6]
    o_ref[...] = (y * jax.nn.sigmoid(y)).astype(o_ref.dtype)


def _conv_silu(main, conv_prm, batch, seq):
    t = main.shape[0]
    ts = _tile(seq, 512)
    tc = 1024
    nsb = seq // ts
    col0 = MAIN_XBC // tc
    halo = BF16_SUBLANE_TILE
    rpb = ts // halo
    last_halo = t // halo - 1
    blocks = [((halo, tc), BF16), ((ts, tc), BF16), ((halo, tc), BF16), ((8, tc), F32), ((ts, tc), BF16)]
    return pl.pallas_call(
        functools.partial(_conv_kernel, n_seq_blocks=nsb),
        grid=(batch, nsb, SSD_CONV_DIM // tc),
        in_specs=[
            pl.BlockSpec((halo, tc), lambda b, j, c: (jnp.maximum((b * nsb + j) * rpb - 1, 0), col0 + c)),
            pl.BlockSpec((ts, tc), lambda b, j, c: (b * nsb + j, col0 + c)),
            pl.BlockSpec((halo, tc), lambda b, j, c: (jnp.minimum((b * nsb + j + 1) * rpb, last_halo), col0 + c)),
            pl.BlockSpec((8, tc), lambda b, j, c: (0, c)),
        ],
        out_specs=pl.BlockSpec((ts, tc), lambda b, j, c: (b * nsb + j, c)),
        out_shape=jax.ShapeDtypeStruct((t, SSD_CONV_DIM), BF16),
        compiler_params=_params(("arbitrary",) * 3, blocks, temp_bytes=8 * _nbytes((ts, tc), F32)),
        name="conv_silu",
    )(main, main, main, conv_prm)


def _ssd_kernel(xf_ref, bf_ref, cf_ref, dtf_ref, xb_ref, bb_ref, cb_ref, dtb_ref, prm_ref,
                yf_ref, yb_ref, st_ref, col_ref, row_ref, *, groups_per_step, expand_inter, use_exp2):
    q = SSD_CHUNK
    k = pl.program_id(1)
    gi = pl.program_id(2)
    half = SSD_N_HEADS

    @pl.when(jnp.logical_and(k == 0, gi == 0))
    def _zero_state():
        st_ref[...] = jnp.zeros(st_ref.shape, st_ref.dtype)

    @pl.when(gi == 0)
    def _per_chunk_decay_terms():
        lane = lax.broadcasted_iota(jnp.int32, (q, 2 * half), 1)
        is_fwd = lane < half
        x = jnp.where(is_fwd, dtf_ref[...], dtb_ref[...]) + prm_ref[0:1, :]
        dt = jnp.maximum(x, 0.0) + jnp.log1p(jnp.exp(-jnp.abs(x)))
        a = -jnp.exp(prm_ref[1:2, :]) * dt
        r = lax.broadcasted_iota(jnp.int32, (q, q), 0)
        c = lax.broadcasted_iota(jnp.int32, (q, q), 1)
        tril = (r >= c).astype(BF16)
        triu = (r <= c).astype(BF16)
        a_hi = a.astype(BF16)
        a_lo = (a - a_hi.astype(F32)).astype(BF16)
        p = jnp.where(is_fwd, _dot(tril, a_hi) + _dot(tril, a_lo), _dot(triu, a_hi) + _dot(triu, a_lo))
        p_tot = jnp.where(is_fwd[0:1], p[q - 1:q, :], p[0:1, :])
        p_seg = p * LOG2_E if use_exp2 else p
        col_ref[0] = p_seg
        col_ref[1] = jnp.exp(p)
        row_ref[0] = p_seg.T
        row_ref[1] = dt.T
        row_ref[2] = (dt * jnp.exp(p_tot - p)).T
        row_ref[3] = jnp.broadcast_to(jnp.exp(p_tot), (q, 2 * half)).T

    ri = lax.broadcasted_iota(jnp.int32, (q, q), 0)
    ci = lax.broadcasted_iota(jnp.int32, (q, q), 1)
    masks = (ri >= ci, ri <= ci)
    low_lanes = ci < SSD_HEAD_DIM
    dirs = ((xf_ref, bf_ref, cf_ref, yf_ref), (xb_ref, bb_ref, cb_ref, yb_ref))
    hd = SSD_HEAD_DIM
    zero_b = jnp.zeros((q, 2 * hd), BF16)
    if expand_inter:
        er = lax.broadcasted_iota(jnp.int32, (2 * half, SSD_GROUP_WIDTH), 0)
        ec = lax.broadcasted_iota(jnp.int32, (2 * half, SSD_GROUP_WIDTH), 1)
        expand = tuple((er == d * half + ec // hd).astype(BF16) for d in range(2))
    for gg in range(groups_per_step):
        g = gi * groups_per_step + gg
        shift = lax.rem(2 * half - g * SSD_HEADS_PER_GROUP, 2 * half)
        p_cols = pltpu.roll(col_ref[0], shift, 1)
        ep_cols = pltpu.roll(col_ref[1], shift, 1)
        for d, (x_ref, b_ref, c_ref, y_ref) in enumerate(dirs):
            bg = b_ref[:, gg * SSD_D_STATE:(gg + 1) * SSD_D_STATE]
            cg = c_ref[:, gg * SSD_D_STATE:(gg + 1) * SSD_D_STATE]
            cb_masked = jnp.where(masks[d], _dot_nt(cg, bg), 0.0)
            cg32 = cg.astype(F32)
            bt = bg.astype(F32).T
            r0 = pl.multiple_of(d * half + g * SSD_HEADS_PER_GROUP, SSD_HEADS_PER_GROUP)
            p_rows = row_ref[0, pl.ds(r0, SSD_HEADS_PER_GROUP), :]
            dt_rows = row_ref[1, pl.ds(r0, SSD_HEADS_PER_GROUP), :]
            w_rows = row_ref[2, pl.ds(r0, SSD_HEADS_PER_GROUP), :]
            dec_rows = row_ref[3, pl.ds(r0, SSD_HEADS_PER_GROUP), :]
            if expand_inter:
                z_all = _dot(cg, st_ref[d, g].astype(BF16))
                ep_hi = ep_cols.astype(BF16)
                ep_lo = (ep_cols - ep_hi.astype(F32)).astype(BF16)
                e_all = _dot(ep_hi, expand[d]) + _dot(ep_lo, expand[d])
                inter = e_all * z_all

            def head_terms(hh):
                li = d * half + hh
                seg = p_cols[:, li:li + 1] - p_rows[hh:hh + 1, :]
                decay = jnp.exp2(jnp.minimum(seg, 0.0)) if use_exp2 else jnp.exp(jnp.minimum(seg, 0.0))
                l_mat = decay * (cb_masked * dt_rows[hh:hh + 1, :])
                btw = bt * w_rows[hh:hh + 1, :]
                c_scaled = None if expand_inter else (cg32 * ep_cols[:, li:li + 1]).astype(BF16)
                return l_mat.astype(BF16), c_scaled, btw.astype(BF16)

            for hp in range(SSD_HEADS_PER_GROUP // 2):
                h0, h1 = 2 * hp, 2 * hp + 1
                lo = (gg * SSD_HEADS_PER_GROUP + h0) * hd
                xp = x_ref[:, lo:lo + 2 * hd]
                x0 = jnp.where(low_lanes, xp, zero_b)
                x1 = jnp.where(low_lanes, zero_b, xp)
                sp = st_ref[d, g, :, h0 * hd:(h0 + 2) * hd]
                spb = None if expand_inter else sp.astype(BF16)
                l0, c0, w0 = head_terms(h0)
                l1, c1, w1 = head_terms(h1)
                x_rhs = jnp.concatenate([x0, x1], axis=0)
                if expand_inter:
                    y_pair = _dot(jnp.concatenate([l0, l1], axis=1), x_rhs) + inter[:, h0 * hd:(h0 + 2) * hd]
                else:
                    lhs = jnp.concatenate([l0, l1, c0, c1], axis=1)
                    rhs = jnp.concatenate([x_rhs, jnp.where(low_lanes, spb, zero_b), jnp.where(low_lanes, zero_b, spb)], axis=0)
                    y_pair = _dot(lhs, rhs)
                y_ref[:, lo:lo + 2 * hd] = y_pair.astype(y_ref.dtype)
                dec = jnp.where(low_lanes[0:1], dec_rows[h0:h0 + 1, :], dec_rows[h1:h1 + 1, :])
                st_ref[d, g, :, h0 * hd:(h0 + 2) * hd] = sp * dec + _dot(jnp.concatenate([w0, w1], axis=1), x_rhs)


def _ssd_scan(act, small, ssd_prm, batch, seq, gps=2, expand_inter=False, use_exp2=False):
    t = act.shape[0]
    q = SSD_CHUNK
    nc = seq // q
    xw = gps * SSD_GROUP_WIDTH
    bw = gps * SSD_D_STATE
    b0 = ACT_B // bw
    c0 = ACT_C // bw
    dtc = SMALL_DT // LANE

    def fwd(col0):
        return lambda b, k, gi: (b * nc + k, col0 + gi)

    def bwd(col0):
        return lambda b, k, gi: (b * nc + nc - 1 - k, col0 + gi)

    blocks = 2 * [((q, xw), BF16), ((q, bw), BF16), ((q, bw), BF16), ((q, LANE), F32), ((q, xw), BF16)] + [((8, LANE), F32)]
    scratch = [
        pltpu.VMEM((2, SSD_N_GROUPS, SSD_D_STATE, SSD_GROUP_WIDTH), F32),
        pltpu.VMEM((2, q, LANE), F32),
        pltpu.VMEM((4, LANE, q), F32),
    ]
    scratch_bytes = _nbytes((2, SSD_N_GROUPS, SSD_D_STATE, SSD_GROUP_WIDTH), F32) + 6 * _nbytes((q, LANE), F32)
    return pl.pallas_call(
        functools.partial(_ssd_kernel, groups_per_step=gps, expand_inter=expand_inter, use_exp2=use_exp2),
        grid=(batch, nc, SSD_N_GROUPS // gps),
        in_specs=[
            pl.BlockSpec((q, xw), fwd(0)), pl.BlockSpec((q, bw), fwd(b0)), pl.BlockSpec((q, bw), fwd(c0)),
            pl.BlockSpec((q, LANE), lambda b, k, gi: (b * nc + k, dtc)),
            pl.BlockSpec((q, xw), bwd(0)), pl.BlockSpec((q, bw), bwd(b0)), pl.BlockSpec((q, bw), bwd(c0)),
            pl.BlockSpec((q, LANE), lambda b, k, gi: (b * nc + nc - 1 - k, dtc)),
            pl.BlockSpec((8, LANE), lambda b, k, gi: (0, 0)),
        ],
        out_specs=[pl.BlockSpec((q, xw), fwd(0)), pl.BlockSpec((q, xw), bwd(0))],
        out_shape=[jax.ShapeDtypeStruct((t, SSD_D_INNER), BF16)] * 2,
        scratch_shapes=scratch,
        compiler_params=_params(("arbitrary",) * 3, blocks, scratch_bytes, temp_bytes=64 * _nbytes((q, q), F32)),
        name="ssd_scan",
    )(act, act, act, small, act, act, act, small, ssd_prm)


def _ssd_finish_kernel(yf_ref, yb_ref, xs_ref, z_ref, prm_ref, o_ref):
    z = z_ref[...].astype(F32)
    y = yf_ref[...].astype(F32) + yb_ref[...].astype(F32) + prm_ref[0:1, :] * xs_ref[...].astype(F32)
    y = y * (z * jax.nn.sigmoid(z))
    gw = SSD_GROUP_WIDTH
    for g in range(SSD_N_GROUPS):
        yg = y[:, g * gw:(g + 1) * gw]
        ms = jnp.mean(yg * yg, axis=-1, keepdims=True)
        o_ref[:, g * gw:(g + 1) * gw] = (yg * lax.rsqrt(ms + RMS_EPS) * prm_ref[1:2, g * gw:(g + 1) * gw]).astype(o_ref.dtype)


def _ssd_finish(yf, yb, act, main, fin_prm):
    t = yf.shape[0]
    ts = _tile(t, 256)
    di = SSD_D_INNER
    row_spec = pl.BlockSpec((ts, di), lambda i: (i, 0))
    blocks = 5 * [((ts, di), BF16)] + [((8, di), F32)]
    return pl.pallas_call(
        _ssd_finish_kernel,
        grid=(t // ts,),
        in_specs=[row_spec, row_spec, row_spec, row_spec, pl.BlockSpec((8, di), lambda i: (0, 0))],
        out_specs=row_spec,
        out_shape=jax.ShapeDtypeStruct((t, di), BF16),
        compiler_params=_params(("arbitrary",), blocks, temp_bytes=4 * _nbytes((ts, di), F32)),
        name="ssd_finish",
    )(yf, yb, act, main, fin_prm)


def _rope_lanes(t, tab):
    return (t * tab[:, 0:LANE] + pltpu.roll(t, LANE - MLA_ROPE_DIM // 2, 1) * tab[:, LANE:2 * LANE]
            + pltpu.roll(t, MLA_ROPE_DIM // 2, 1) * tab[:, 2 * LANE:3 * LANE])


def _rms_rows(x, w):
    return x * lax.rsqrt(jnp.mean(x * x, axis=-1, keepdims=True) + RMS_EPS) * w


def _qproj_kernel(x_ref, nw_ref, w_ref, tab_ref, qt_ref, *, scale):
    xt = x_ref[...].T
    tm = xt.shape[1]
    inv = lax.rsqrt(jnp.mean(xt * xt, axis=0, keepdims=True) + RMS_EPS)
    nw = jnp.concatenate([nw_ref[...]] * (tm // LANE), axis=1)
    xnt = (xt * inv * nw).astype(BF16)
    half = MLA_ROPE_DIM // 2
    cos = tab_ref[0:half, :]
    sin = tab_ref[half:2 * half, :]
    r0 = MLA_NOPE_DIM
    for h in range(MLA_N_HEADS):
        qt = _dot(w_ref[h * MLA_QK_PAD:(h + 1) * MLA_QK_PAD, :], xnt)
        x1 = qt[r0:r0 + half]
        x2 = qt[r0 + half:r0 + 2 * half]
        qt_ref[0, h, 0:r0, :] = (qt[0:r0] * scale).astype(qt_ref.dtype)
        qt_ref[0, h, r0:r0 + half, :] = ((x1 * cos - x2 * sin) * scale).astype(qt_ref.dtype)
        qt_ref[0, h, r0 + half:r0 + 2 * half, :] = ((x2 * cos + x1 * sin) * scale).astype(qt_ref.dtype)
        qt_ref[0, h, r0 + 2 * half:MLA_QK_PAD, :] = jnp.zeros((MLA_QK_PAD - MLA_QK_DIM, tm), qt_ref.dtype)


def _kvproj_kernel(x_ref, kr_ref, nw_ref, wk_ref, wvt_ref, tab_ref, k_ref, vt_ref, *, tk):
    cn = _rms_rows(x_ref[...], nw_ref[0:1, :])
    tm = cn.shape[0]
    k_all = _dot(cn.astype(BF16), wk_ref[...])
    vt_all = _dot(wvt_ref[...], cn.T.astype(BF16))
    k_rope = _rope_lanes(kr_ref[...], tab_ref[...]).astype(k_ref.dtype)
    pad_rows = FLASH_VT_ROWS - MLA_V_DIM
    ones_rows = (lax.broadcasted_iota(jnp.int32, (pad_rows, tk), 0) == 0).astype(vt_ref.dtype)
    for h in range(MLA_N_HEADS):
        k_ref[0, h, :, 0:LANE] = k_all[:, h * MLA_NOPE_DIM:(h + 1) * MLA_NOPE_DIM].astype(k_ref.dtype)
        k_ref[0, h, :, LANE:2 * LANE] = k_rope
        for c in range(tm // tk):
            vt_ref[0, h, c, 0:MLA_V_DIM, :] = vt_all[h * MLA_V_DIM:(h + 1) * MLA_V_DIM, c * tk:(c + 1) * tk].astype(vt_ref.dtype)
            vt_ref[0, h, c, MLA_V_DIM:FLASH_VT_ROWS, :] = ones_rows


def _mla_proj(small, q_nw_lanes, wq_t, kv_nw, wk, wv_t, rope_tab, rope_tab_t, batch, seq, flash_tk=FLASH_TK):
    t = small.shape[0]
    tk = _tile(seq, flash_tk)
    tm = _tile(seq, max(512, tk))
    nsb = seq // tm
    nh = MLA_N_HEADS
    qw = nh * MLA_QK_PAD
    kw = nh * MLA_NOPE_DIM
    vw = nh * MLA_V_DIM
    q_blocks = [((tm, MLA_Q_RANK), F32), ((MLA_Q_RANK, LANE), F32), ((qw, MLA_Q_RANK), BF16), ((MLA_ROPE_DIM, tm), F32),
                ((nh, MLA_QK_PAD, tm), BF16)]
    qt = pl.pallas_call(
        functools.partial(_qproj_kernel, scale=MLA_QK_DIM ** -0.5 * LOG2_E),
        grid=(t // tm,),
        in_specs=[pl.BlockSpec((tm, MLA_Q_RANK), lambda i: (i, SMALL_Q // MLA_Q_RANK)),
                  pl.BlockSpec((MLA_Q_RANK, LANE), lambda i: (0, 0)),
                  pl.BlockSpec((qw, MLA_Q_RANK), lambda i: (0, 0)),
                  pl.BlockSpec((MLA_ROPE_DIM, tm), lambda i: (0, i % nsb))],
        out_specs=pl.BlockSpec((1, nh, MLA_QK_PAD, tm), lambda i: (i // nsb, 0, 0, i % nsb)),
        out_shape=jax.ShapeDtypeStruct((batch, nh, MLA_QK_PAD, seq), BF16),
        compiler_params=_params(("arbitrary",), q_blocks, temp_bytes=6 * _nbytes((tm, MLA_Q_RANK), F32)),
        name="mla_q_proj",
    )(small, q_nw_lanes, wq_t, rope_tab_t)
    kv_blocks = [((tm, MLA_KV_RANK), F32), ((tm, LANE), F32), ((8, MLA_KV_RANK), F32), ((MLA_KV_RANK, kw), BF16),
                 ((vw, MLA_KV_RANK), BF16), ((tm, 3 * LANE), F32), ((nh, tm, MLA_QK_PAD), BF16), ((nh, tm // tk, FLASH_VT_ROWS, tk), BF16)]
    k, vt = pl.pallas_call(
        functools.partial(_kvproj_kernel, tk=tk),
        grid=(t // tm,),
        in_specs=[pl.BlockSpec((tm, MLA_KV_RANK), lambda i: (i, SMALL_CKV // MLA_KV_RANK)),
                  pl.BlockSpec((tm, LANE), lambda i: (i, SMALL_KROPE // LANE)),
                  pl.BlockSpec((8, MLA_KV_RANK), lambda i: (0, 0)),
                  pl.BlockSpec((MLA_KV_RANK, kw), lambda i: (0, 0)),
                  pl.BlockSpec((vw, MLA_KV_RANK), lambda i: (0, 0)),
                  pl.BlockSpec((tm, 3 * LANE), lambda i: (i % nsb, 0))],
        out_specs=[pl.BlockSpec((1, nh, tm, MLA_QK_PAD), lambda i: (i // nsb, 0, i % nsb, 0)),
                   pl.BlockSpec((1, nh, tm // tk, FLASH_VT_ROWS, tk), lambda i: (i // nsb, 0, i % nsb, 0, 0))],
        out_shape=[jax.ShapeDtypeStruct((batch, nh, seq, MLA_QK_PAD), BF16),
                   jax.ShapeDtypeStruct((batch, nh, seq // tk, FLASH_VT_ROWS, tk), BF16)],
        compiler_params=_params(("arbitrary",), kv_blocks, temp_bytes=2 * _nbytes((tm, kw), F32) + 4 * _nbytes((tm, MLA_KV_RANK), F32)),
        name="mla_kv_proj",
    )(small, small, kv_nw, wk, wv_t, rope_tab)
    return qt, k, vt


def _flash_kernel(qt_ref, k_ref, vt_ref, o_ref, s0_ref, s1_ref, p0_ref, p1_ref, acc_ref, *, unroll):
    tq = qt_ref.shape[3]
    n_kv, _, tk = vt_ref.shape[2:]

    def qk(j, s_ref):
        s = _dot(k_ref[0, 0, pl.ds(pl.multiple_of(j * tk, tk), tk), :], qt_ref[0, 0])
        s_ref[...] = s
        return jnp.max(s, axis=0, keepdims=True)

    def soft(m, cm, s_ref, p_ref):
        m_new = jnp.maximum(m, cm)
        p_ref[...] = jnp.exp2(s_ref[...] - m_new).astype(p_ref.dtype)
        return m_new, jnp.exp2(m - m_new)

    def pv(j, alpha, p_ref):
        acc_ref[...] = alpha * acc_ref[...] + _dot(vt_ref[0, 0, j], p_ref[...])

    def step(j, m, alpha_prev, cm, s_cur, s_nxt, p_prev, p_cur):
        cm_next = qk(j + 1, s_nxt)
        pv(j - 1, alpha_prev, p_prev)
        m, alpha = soft(m, cm, s_cur, p_cur)
        return m, alpha, cm_next

    acc_ref[...] = jnp.zeros(acc_ref.shape, acc_ref.dtype)
    m = jnp.full((1, tq), -jnp.inf, F32)
    m, alpha = soft(m, qk(0, s0_ref), s0_ref, p0_ref)
    cm = qk(1, s1_ref)

    def body(i, carry):
        m, alpha, cm = carry
        j = 2 * i + 1
        m, alpha, cm = step(j, m, alpha, cm, s1_ref, s0_ref, p0_ref, p1_ref)
        m, alpha, cm = step(j + 1, m, alpha, cm, s0_ref, s1_ref, p1_ref, p0_ref)
        return m, alpha, cm

    m, alpha, cm = lax.fori_loop(0, (n_kv - 2) // 2, body, (m, alpha, cm), unroll=unroll)
    pv(n_kv - 2, alpha, p0_ref)
    m, alpha = soft(m, cm, s1_ref, p1_ref)
    pv(n_kv - 1, alpha, p1_ref)
    o_ref[...] = (acc_ref[0:MLA_V_DIM, :] / acc_ref[MLA_V_DIM:MLA_V_DIM + 1, :]).T.astype(o_ref.dtype)


def _flash_attention(qt, k, vt, flash_tq=FLASH_TQ, unroll=FLASH_UNROLL):
    batch, nh, seq, _ = k.shape
    n_kv, vt_rows, tk = vt.shape[2:]
    assert n_kv >= 2 and n_kv % 2 == 0, n_kv
    tq = _tile(seq, flash_tq)
    nq = seq // tq
    blocks = [((MLA_QK_PAD, tq), BF16), ((seq, MLA_QK_PAD), BF16), ((n_kv, vt_rows, tk), BF16), ((tq, MLA_V_DIM), BF16)]
    scratch = [pltpu.VMEM((tk, tq), F32), pltpu.VMEM((tk, tq), F32), pltpu.VMEM((tk, tq), BF16), pltpu.VMEM((tk, tq), BF16),
               pltpu.VMEM((vt_rows, tq), F32)]
    scratch_bytes = 3 * _nbytes((tk, tq), F32) + _nbytes((vt_rows, tq), F32)
    return pl.pallas_call(
        functools.partial(_flash_kernel, unroll=unroll),
        grid=(batch, nh, nq),
        in_specs=[
            pl.BlockSpec((1, 1, MLA_QK_PAD, tq), lambda b, h, i: (b, h, 0, i)),
            pl.BlockSpec((1, 1, seq, MLA_QK_PAD), lambda b, h, i: (b, h, 0, 0)),
            pl.BlockSpec((1, 1, n_kv, vt_rows, tk), lambda b, h, i: (b, h, 0, 0, 0)),
        ],
        out_specs=pl.BlockSpec((tq, MLA_V_DIM), lambda b, h, i: (b * nq + i, h)),
        out_shape=jax.ShapeDtypeStruct((batch * seq, nh * MLA_V_DIM), BF16),
        scratch_shapes=scratch,
        compiler_params=_params(("arbitrary",) * 3, blocks, scratch_bytes, temp_bytes=4 * _nbytes((tk, tq), F32)),
        name="mla_flash_attention",
    )(qt, k, vt)


def _merge_kernel(a1_ref, w1_ref, a2_ref, w2_ref, g1_ref, g2_ref, o_ref):
    y1 = _dot(a1_ref[...], w1_ref[...])
    y2 = _dot(a2_ref[...], w2_ref[...])
    g1 = jax.nn.sigmoid(g1_ref[...].astype(F32))
    g2 = jax.nn.sigmoid(g2_ref[...].astype(F32))
    o_ref[...] = (g1 * y1 + g2 * y2).astype(o_ref.dtype)


def _merge(y_ssd_n, w_ssd_out, attn, w_mla_out, main):
    t = y_ssd_n.shape[0]
    tm = _tile(t, 512)
    tn = 512
    k1, k2 = SSD_D_INNER, MLA_N_HEADS * MLA_V_DIM
    blocks = [((tm, k1), BF16), ((k1, tn), BF16), ((tm, k2), BF16), ((k2, tn), BF16), ((tm, tn), BF16), ((tm, tn), BF16), ((tm, tn), BF16)]
    return pl.pallas_call(
        _merge_kernel,
        grid=(D_MODEL // tn, t // tm),
        in_specs=[
            pl.BlockSpec((tm, k1), lambda j, i: (i, 0)), pl.BlockSpec((k1, tn), lambda j, i: (0, j)),
            pl.BlockSpec((tm, k2), lambda j, i: (i, 0)), pl.BlockSpec((k2, tn), lambda j, i: (0, j)),
            pl.BlockSpec((tm, tn), lambda j, i: (i, MAIN_GSSD // tn + j)),
            pl.BlockSpec((tm, tn), lambda j, i: (i, MAIN_GMLA // tn + j)),
        ],
        out_specs=pl.BlockSpec((tm, tn), lambda j, i: (i, j)),
        out_shape=jax.ShapeDtypeStruct((t, D_MODEL), BF16),
        compiler_params=_params(("arbitrary",) * 2, blocks, temp_bytes=4 * _nbytes((tm, tn), F32)),
        name="branch_merge",
    )(y_ssd_n, w_ssd_out, attn, w_mla_out, main, main)


def _layer_norm_rows(r, prm_ref):
    mu = jnp.mean(r, axis=-1, keepdims=True)
    c = r - mu
    var = jnp.mean(c * c, axis=-1, keepdims=True)
    return c * lax.rsqrt(var + LN_EPS) * prm_ref[0:1, :] + prm_ref[1:2, :]


def _out_ln_kernel(a_ref, w_ref, h_ref, prm_ref, of_ref, ob_ref):
    y = _layer_norm_rows(DEEPNORM_ALPHA * h_ref[...] + _dot(a_ref[...], w_ref[...]), prm_ref)
    of_ref[...] = y
    ob_ref[...] = y.astype(ob_ref.dtype)


def _out_ln(merged, w_out, h, ln_prm):
    t = h.shape[0]
    tm = _tile(t, 256)
    d = D_MODEL
    blocks = [((tm, d), BF16), ((d, d), BF16), ((tm, d), F32), ((8, d), F32), ((tm, d), F32), ((tm, d), BF16)]
    row = lambda i: (i, 0)
    return pl.pallas_call(
        _out_ln_kernel,
        grid=(t // tm,),
        in_specs=[pl.BlockSpec((tm, d), row), pl.BlockSpec((d, d), lambda i: (0, 0)), pl.BlockSpec((tm, d), row),
                  pl.BlockSpec((8, d), lambda i: (0, 0))],
        out_specs=[pl.BlockSpec((tm, d), row), pl.BlockSpec((tm, d), row)],
        out_shape=[jax.ShapeDtypeStruct((t, d), F32), jax.ShapeDtypeStruct((t, d), BF16)],
        compiler_params=_params(("arbitrary",), blocks, temp_bytes=3 * _nbytes((tm, d), F32)),
        name="mixer_out_ln",
    )(merged, w_out, h, ln_prm)


def _ffn_up_kernel(a_ref, wg_ref, wu_ref, o_ref):
    a = a_ref[...]
    gate = _dot(a, wg_ref[...])
    o_ref[...] = (gate * jax.nn.sigmoid(gate) * _dot(a, wu_ref[...])).astype(o_ref.dtype)


def _ffn_up(hb, wg, wu):
    t = hb.shape[0]
    tm = _tile(t, 1024)
    tn = 512
    d = D_MODEL
    blocks = [((tm, d), BF16), ((d, tn), BF16), ((d, tn), BF16), ((tm, tn), BF16)]
    return pl.pallas_call(
        _ffn_up_kernel,
        grid=(FFN_HIDDEN // tn, t // tm),
        in_specs=[pl.BlockSpec((tm, d), lambda j, i: (i, 0)), pl.BlockSpec((d, tn), lambda j, i: (0, j)),
                  pl.BlockSpec((d, tn), lambda j, i: (0, j))],
        out_specs=pl.BlockSpec((tm, tn), lambda j, i: (i, j)),
        out_shape=jax.ShapeDtypeStruct((t, FFN_HIDDEN), BF16),
        compiler_params=_params(("arbitrary",) * 2, blocks, temp_bytes=3 * _nbytes((tm, tn), F32)),
        name="ffn_up",
    )(hb, wg, wu)


def _ffn_down_ln_kernel(a_ref, w_ref, h_ref, prm_ref, of_ref, ob_ref, acc_ref):
    kk = pl.program_id(1)

    @pl.when(kk == 0)
    def _init():
        acc_ref[...] = DEEPNORM_ALPHA * h_ref[...]

    acc_ref[...] += _dot(a_ref[...], w_ref[...])

    @pl.when(kk == pl.num_programs(1) - 1)
    def _finish():
        y = _layer_norm_rows(acc_ref[...], prm_ref)
        of_ref[...] = y
        ob_ref[...] = y.astype(ob_ref.dtype)


def _ffn_down_ln(act, w_down, h, ln_prm, tm=512, tk=512):
    t = h.shape[0]
    tm = _tile(t, tm)
    d = D_MODEL
    blocks = [((tm, tk), BF16), ((tk, d), BF16), ((tm, d), F32), ((8, d), F32), ((tm, d), F32), ((tm, d), BF16)]
    row = lambda i, kk: (i, 0)
    return pl.pallas_call(
        _ffn_down_ln_kernel,
        grid=(t // tm, FFN_HIDDEN // tk),
        in_specs=[pl.BlockSpec((tm, tk), lambda i, kk: (i, kk)), pl.BlockSpec((tk, d), lambda i, kk: (kk, 0)),
                  pl.BlockSpec((tm, d), row), pl.BlockSpec((8, d), lambda i, kk: (0, 0))],
        out_specs=[pl.BlockSpec((tm, d), row), pl.BlockSpec((tm, d), row)],
        out_shape=[jax.ShapeDtypeStruct((t, d), F32), jax.ShapeDtypeStruct((t, d), BF16)],
        scratch_shapes=[pltpu.VMEM((tm, d), F32)],
        compiler_params=_params(("arbitrary",) * 2, blocks, _nbytes((tm, d), F32), temp_bytes=2 * _nbytes((tm, d), F32)),
        name="ffn_down_ln",
    )(act, w_down, h, ln_prm)


def _rows8(*rows):
    n = rows[0].shape[0]
    pad = jnp.zeros((8 - len(rows), n), F32)
    return jnp.concatenate([jnp.stack([r.astype(F32) for r in rows]), pad], axis=0)


def _rope_table(seq):
    pos = jnp.arange(seq, dtype=F32)
    inv_freq = ROPE_THETA ** (-jnp.arange(0, MLA_ROPE_DIM, 2, dtype=F32) / MLA_ROPE_DIM)
    ang = pos[:, None] * inv_freq[None, :]
    cos, sin = jnp.cos(ang), jnp.sin(ang)
    half = MLA_ROPE_DIM // 2
    z = lambda n: jnp.zeros((seq, n), F32)
    lanes = jnp.concatenate([cos, cos, z(LANE - 2 * half), -sin, z(LANE - half), z(half), sin, z(LANE - 2 * half)], axis=1)
    return lanes, jnp.concatenate([cos.T, sin.T], axis=0)


def _q_weight_t(w_uq):
    r = w_uq.shape[0]
    w = w_uq.reshape(r, MLA_N_HEADS, MLA_QK_DIM)
    w = jnp.pad(w, ((0, 0), (0, 0), (0, MLA_QK_PAD - MLA_QK_DIM)))
    return w.reshape(r, MLA_N_HEADS * MLA_QK_PAD).T.astype(BF16)


def _kv_weights(w_ukv):
    r = w_ukv.shape[0]
    w = w_ukv.reshape(r, MLA_N_HEADS, MLA_NOPE_DIM + MLA_V_DIM)
    wk = w[:, :, :MLA_NOPE_DIM].reshape(r, MLA_N_HEADS * MLA_NOPE_DIM)
    wv = w[:, :, MLA_NOPE_DIM:].reshape(r, MLA_N_HEADS * MLA_V_DIM)
    return wk.astype(BF16), wv.T.astype(BF16)


LAYER_VARIANTS = (
    dict(ssd_gps=4, ssd_expand=True, ssd_exp2=False, flash_tq=512, flash_tk=256, flash_unroll=4, down_tm=512, down_tk=1408),
    dict(ssd_gps=8, ssd_expand=True, ssd_exp2=False, flash_tq=512, flash_tk=256, flash_unroll=8, down_tm=512, down_tk=1408),
    dict(ssd_gps=4, ssd_expand=True, ssd_exp2=True, flash_tq=512, flash_tk=256, flash_unroll=31, down_tm=512, down_tk=1408),
    dict(ssd_gps=8, ssd_expand=True, ssd_exp2=True, flash_tq=1024, flash_tk=256, flash_unroll=4, down_tm=512, down_tk=1408),
)


def kernel(x, w_in, conv_w, conv_b, ssd_a_log, ssd_dt_bias, ssd_d, ssd_norm_w, w_ssd_out, mla_q_norm_w, w_uq,
           mla_kv_norm_w, w_ukv, w_mla_out, w_out, ln1_g, ln1_b, w_ffn_gate, w_ffn_up, w_ffn_down, ln2_g, ln2_b):
    batch, seq, d = x.shape
    assert d == D_MODEL and seq % SSD_CHUNK == 0
    t = batch * seq
    hf = x.reshape(t, d)
    hb = hf.astype(BF16)
    rope_tab, rope_tab_t = _rope_table(seq)
    dt0 = MAIN_GSSD + 2 * SSD_N_HEADS
    q0 = dt0
    kv0 = q0 + MLA_Q_RANK
    kr0 = kv0 + MLA_KV_RANK
    g0 = kr0 + MLA_ROPE_DIM
    for l in range(w_in.shape[0]):
        cfg = LAYER_VARIANTS[l % len(LAYER_VARIANTS)]
        wi = w_in[l]
        w_main = jnp.concatenate([wi[:, :MAIN_GSSD], wi[:, g0:]], axis=1).astype(BF16)
        w_small = jnp.concatenate(
            [wi[:, q0:kv0], wi[:, kv0:kr0], wi[:, MAIN_GSSD:dt0], wi[:, kr0:g0], jnp.zeros((d, LANE - MLA_ROPE_DIM), wi.dtype)],
            axis=1).astype(BF16)
        main = _matmul(hb, w_main, BF16, 1024, "in_proj_main")
        small = _matmul(hb, w_small, F32, SMALL_WIDTH, "in_proj_small")

        act = _conv_silu(main, _rows8(*[conv_w[l, i] for i in range(SSD_CONV_WIDTH)], conv_b[l]), batch, seq)
        yf, yb = _ssd_scan(act, small, _rows8(ssd_dt_bias[l].reshape(-1), ssd_a_log[l].reshape(-1)), batch, seq,
                           gps=cfg['ssd_gps'], expand_inter=cfg['ssd_expand'], use_exp2=cfg['ssd_exp2'])
        y_ssd_n = _ssd_finish(yf, yb, act, main, _rows8(jnp.repeat(ssd_d[l], SSD_HEAD_DIM), ssd_norm_w[l]))

        wk, wv_t = _kv_weights(w_ukv[l])
        q_nw_lanes = jnp.broadcast_to(mla_q_norm_w[l].astype(F32)[:, None], (MLA_Q_RANK, LANE))
        qt, k, vt = _mla_proj(small, q_nw_lanes, _q_weight_t(w_uq[l]), _rows8(mla_kv_norm_w[l]), wk, wv_t,
                              rope_tab, rope_tab_t, batch, seq, flash_tk=cfg['flash_tk'])
        attn = _flash_attention(qt, k, vt, flash_tq=cfg['flash_tq'], unroll=cfg['flash_unroll'])

        merged = _merge(y_ssd_n, w_ssd_out[l].astype(BF16), attn, w_mla_out[l].astype(BF16), main)
        hf, hb = _out_ln(merged, w_out[l].astype(BF16), hf, _rows8(ln1_g[l], ln1_b[l]))
        ffn_act = _ffn_up(hb, w_ffn_gate[l].astype(BF16), w_ffn_up[l].astype(BF16))
        hf, hb = _ffn_down_ln(ffn_act, w_ffn_down[l].astype(BF16), hf, _rows8(ln2_g[l], ln2_b[l]), tm=cfg['down_tm'], tk=cfg['down_tk'])
    return hf.reshape(batch, seq, d)
```

```python
import functools

import jax
import jax.numpy as jnp
from jax import lax
from jax.experimental import pallas as pl
from jax.experimental.pallas import tpu as pltpu

F32 = jnp.float32
BF16 = jnp.bfloat16

D_MODEL = 2048
DEPTH = 4
SSD_D_INNER = 4096
SSD_HEAD_DIM = 64
SSD_N_HEADS = 64
SSD_N_GROUPS = 8
SSD_HEADS_PER_GROUP = 8
SSD_D_STATE = 128
SSD_CONV_WIDTH = 5
SSD_CHUNK = 128
SSD_GROUP_WIDTH = SSD_HEADS_PER_GROUP * SSD_HEAD_DIM
SSD_BC_WIDTH = SSD_N_GROUPS * SSD_D_STATE
SSD_CONV_DIM = SSD_D_INNER + 2 * SSD_BC_WIDTH
MLA_N_HEADS = 16
MLA_Q_RANK = 512
MLA_KV_RANK = 512
MLA_NOPE_DIM = 128
MLA_ROPE_DIM = 64
MLA_V_DIM = 128
MLA_QK_DIM = MLA_NOPE_DIM + MLA_ROPE_DIM
MLA_QK_PAD = 256
ROPE_THETA = 10000.0
FLASH_TQ = 512
FLASH_TK = 256
FLASH_VT_ROWS = MLA_V_DIM + 16
LOG2_E = 1.4426950408889634
FFN_HIDDEN = 5632
DEEPNORM_ALPHA = (2 * DEPTH) ** 0.25
RMS_EPS = 1e-6
LN_EPS = 1e-5

MAIN_Z = 0
MAIN_XBC = SSD_D_INNER
MAIN_GSSD = MAIN_XBC + SSD_CONV_DIM
MAIN_GMLA = MAIN_GSSD + D_MODEL
MAIN_WIDTH = MAIN_GMLA + D_MODEL
SMALL_Q = 0
SMALL_CKV = MLA_Q_RANK
SMALL_DT = SMALL_CKV + MLA_KV_RANK
SMALL_KROPE = SMALL_DT + 2 * SSD_N_HEADS
SMALL_WIDTH = SMALL_KROPE + 128
ACT_B = SSD_D_INNER
ACT_C = SSD_D_INNER + SSD_BC_WIDTH

V7X_VMEM_BYTES = 64 * 1024 * 1024
VMEM_LIMIT_CAP = 56 * 1024 * 1024
VMEM_LIMIT_FLOOR = 32 * 1024 * 1024
LANE = 128
BF16_SUBLANE_TILE = 16


def _nbytes(shape, dtype):
    n = 1
    for s in shape:
        n *= s
    return n * jnp.dtype(dtype).itemsize


def _params(semantics, blocks, scratch_bytes=0, temp_bytes=0):
    need = 2 * sum(_nbytes(s, d) for s, d in blocks) + scratch_bytes + temp_bytes
    limit = min(max(need + need // 4, VMEM_LIMIT_FLOOR), VMEM_LIMIT_CAP)
    return pltpu.CompilerParams(dimension_semantics=semantics, vmem_limit_bytes=limit)


def _tile(n, pref):
    t = min(n, pref)
    assert n % t == 0, (n, pref)
    return t


def _dot(a, b):
    return jnp.dot(a, b, preferred_element_type=F32)


def _dot_nt(a, b):
    return lax.dot_general(a, b, (((1,), (1,)), ((), ())), preferred_element_type=F32)


def _mm_kernel(a_ref, w_ref, o_ref):
    o_ref[...] = _dot(a_ref[...], w_ref[...]).astype(o_ref.dtype)


def _matmul(a, w, out_dtype, tn, name):
    m, k = a.shape
    n = w.shape[1]
    tm = _tile(m, 1024)
    tn = _tile(n, tn)
    blocks = [((tm, k), a.dtype), ((k, tn), w.dtype), ((tm, tn), out_dtype)]
    return pl.pallas_call(
        _mm_kernel,
        grid=(n // tn, m // tm),
        in_specs=[pl.BlockSpec((tm, k), lambda j, i: (i, 0)), pl.BlockSpec((k, tn), lambda j, i: (0, j))],
        out_specs=pl.BlockSpec((tm, tn), lambda j, i: (i, j)),
        out_shape=jax.ShapeDtypeStruct((m, n), out_dtype),
        compiler_params=_params(("arbitrary", "arbitrary"), blocks, temp_bytes=_nbytes((tm, tn), F32)),
        name=name,
    )(a, w)


def _conv_kernel(prev_ref, cur_ref, next_ref, w_ref, o_ref, *, n_seq_blocks):
    j = pl.program_id(1)
    cur = cur_ref[...].astype(F32)
    ts = cur.shape[0]
    has_prev = j > 0
    has_next = j < n_seq_blocks - 1
    halo = BF16_SUBLANE_TILE
    p0 = jnp.where(has_prev, prev_ref[halo - 2:halo - 1, :].astype(F32), 0.0)
    p1 = jnp.where(has_prev, prev_ref[halo - 1:halo, :].astype(F32), 0.0)
    n0 = jnp.where(has_next, next_ref[0:1, :].astype(F32), 0.0)
    n1 = jnp.where(has_next, next_ref[1:2, :].astype(F32), 0.0)
    row = lax.broadcasted_iota(jnp.int32, cur.shape, 0)
    xm1 = jnp.where(row == 0, p1, pltpu.roll(cur, 1, 0))
    xm2 = jnp.where(row == 0, p0, jnp.where(row == 1, p1, pltpu.roll(cur, 2, 0)))
    xp1 = jnp.where(row == ts - 1, n0, pltpu.roll(cur, ts - 1, 0))
    xp2 = jnp.where(row == ts - 2, n0, jnp.where(row == ts - 1, n1, pltpu.roll(cur, ts - 2, 0)))
    w = w_ref[...]
    y = xm2 * w[0:1] + xm1 * w[1:2] + cur * w[2:3] + xp1 * w[3:4] + xp2 * w[4:5] + w[5:6]
    o_ref[...] = (y * jax.nn.sigmoid(y)).astype(o_ref.dtype)


def _conv_silu(main, conv_prm, batch, seq):
    t = main.shape[0]
    ts = _tile(seq, 512)
    tc = 1024
    nsb = seq // ts
    col0 = MAIN_XBC // tc
    halo = BF16_SUBLANE_TILE
    rpb = ts // halo
    last_halo = t // halo - 1
    blocks = [((halo, tc), BF16), ((ts, tc), BF16), ((halo, tc), BF16), ((8, tc), F32), ((ts, tc), BF16)]
    return pl.pallas_call(
        functools.partial(_conv_kernel, n_seq_blocks=nsb),
        grid=(batch, nsb, SSD_CONV_DIM // tc),
        in_specs=[
            pl.BlockSpec((halo, tc), lambda b, j, c: (jnp.maximum((b * nsb + j) * rpb - 1, 0), col0 + c)),
            pl.BlockSpec((ts, tc), lambda b, j, c: (b * nsb + j, col0 + c)),
            pl.BlockSpec((halo, tc), lambda b, j, c: (jnp.minimum((b * nsb + j + 1) * rpb, last_halo), col0 + c)),
            pl.BlockSpec((8, tc), lambda b, j, c: (0, c)),
        ],
        out_specs=pl.BlockSpec((ts, tc), lambda b, j, c: (b * nsb + j, c)),
        out_shape=jax.ShapeDtypeStruct((t, SSD_CONV_DIM), BF16),
        compiler_params=_params(("arbitrary",) * 3, blocks, temp_bytes=8 * _nbytes((ts, tc), F32)),
        name="conv_silu",
    )(main, main, main, conv_prm)


def _ssd_kernel(xf_ref, bf_ref, cf_ref, dtf_ref, xb_ref, bb_ref, cb_ref, dtb_ref, prm_ref,
                yf_ref, yb_ref, st_ref, col_ref, row_ref, *, groups_per_step):
    q = SSD_CHUNK
    k = pl.program_id(1)
    gi = pl.program_id(2)
    half = SSD_N_HEADS

    @pl.when(jnp.logical_and(k == 0, gi == 0))
    def _zero_state():
        st_ref[...] = jnp.zeros(st_ref.shape, st_ref.dtype)

    @pl.when(gi == 0)
    def _per_chunk_decay_terms():
        lane = lax.broadcasted_iota(jnp.int32, (q, 2 * half), 1)
        is_fwd = lane < half
        x = jnp.where(is_fwd, dtf_ref[...], dtb_ref[...]) + prm_ref[0:1, :]
        dt = jnp.maximum(x, 0.0) + jnp.log1p(jnp.exp(-jnp.abs(x)))
        a = -jnp.exp(prm_ref[1:2, :]) * dt
        r = lax.broadcasted_iota(jnp.int32, (q, q), 0)
        c = lax.broadcasted_iota(jnp.int32, (q, q), 1)
        tril = (r >= c).astype(BF16)
        triu = (r <= c).astype(BF16)
        a_hi = a.astype(BF16)
        a_lo = (a - a_hi.astype(F32)).astype(BF16)
        p = jnp.where(is_fwd, _dot(tril, a_hi) + _dot(tril, a_lo), _dot(triu, a_hi) + _dot(triu, a_lo))
        p_tot = jnp.where(is_fwd[0:1], p[q - 1:q, :], p[0:1, :])
        p_seg = p * LOG2_E
        col_ref[0] = p_seg
        col_ref[1] = jnp.exp(p)
        row_ref[0] = p_seg.T
        row_ref[1] = dt.T
        row_ref[2] = (dt * jnp.exp(p_tot - p)).T
        row_ref[3] = jnp.broadcast_to(jnp.exp(p_tot), (q, 2 * half)).T

    ri = lax.broadcasted_iota(jnp.int32, (q, q), 0)
    ci = lax.broadcasted_iota(jnp.int32, (q, q), 1)
    masks = (ri >= ci, ri <= ci)
    low_lanes = ci < SSD_HEAD_DIM
    dirs = ((xf_ref, bf_ref, cf_ref, yf_ref), (xb_ref, bb_ref, cb_ref, yb_ref))
    hd = SSD_HEAD_DIM
    zero_b = jnp.zeros((q, 2 * hd), BF16)
    er = lax.broadcasted_iota(jnp.int32, (2 * half, SSD_GROUP_WIDTH), 0)
    ec = lax.broadcasted_iota(jnp.int32, (2 * half, SSD_GROUP_WIDTH), 1)
    expand = tuple((er == d * half + ec // hd).astype(BF16) for d in range(2))
    for gg in range(groups_per_step):
        g = gi * groups_per_step + gg
        shift = lax.rem(2 * half - g * SSD_HEADS_PER_GROUP, 2 * half)
        p_cols = pltpu.roll(col_ref[0], shift, 1)
        ep_cols = pltpu.roll(col_ref[1], shift, 1)
        for d, (x_ref, b_ref, c_ref, y_ref) in enumerate(dirs):
            bg = b_ref[:, gg * SSD_D_STATE:(gg + 1) * SSD_D_STATE]
            cg = c_ref[:, gg * SSD_D_STATE:(gg + 1) * SSD_D_STATE]
            cb_masked = jnp.where(masks[d], _dot_nt(cg, bg), 0.0)
            bt = bg.astype(F32).T
            r0 = pl.multiple_of(d * half + g * SSD_HEADS_PER_GROUP, SSD_HEADS_PER_GROUP)
            p_rows = row_ref[0, pl.ds(r0, SSD_HEADS_PER_GROUP), :]
            dt_rows = row_ref[1, pl.ds(r0, SSD_HEADS_PER_GROUP), :]
            w_rows = row_ref[2, pl.ds(r0, SSD_HEADS_PER_GROUP), :]
            dec_rows = row_ref[3, pl.ds(r0, SSD_HEADS_PER_GROUP), :]
            ep_hi = ep_cols.astype(BF16)
            ep_lo = (ep_cols - ep_hi.astype(F32)).astype(BF16)
            inter = (_dot(ep_hi, expand[d]) + _dot(ep_lo, expand[d])) * _dot(cg, st_ref[d, g].astype(BF16))

            def head_terms(hh):
                li = d * half + hh
                seg = p_cols[:, li:li + 1] - p_rows[hh:hh + 1, :]
                l_mat = jnp.exp2(jnp.minimum(seg, 0.0)) * (cb_masked * dt_rows[hh:hh + 1, :])
                btw = bt * w_rows[hh:hh + 1, :]
                return l_mat.astype(BF16), btw.astype(BF16)

            for hp in range(SSD_HEADS_PER_GROUP // 2):
                h0, h1 = 2 * hp, 2 * hp + 1
                lo = (gg * SSD_HEADS_PER_GROUP + h0) * hd
                xp = x_ref[:, lo:lo + 2 * hd]
                x0 = jnp.where(low_lanes, xp, zero_b)
                x1 = jnp.where(low_lanes, zero_b, xp)
                sp = st_ref[d, g, :, h0 * hd:(h0 + 2) * hd]
                l0, w0 = head_terms(h0)
                l1, w1 = head_terms(h1)
                x_rhs = jnp.concatenate([x0, x1], axis=0)
                y_pair = _dot(jnp.concatenate([l0, l1], axis=1), x_rhs) + inter[:, h0 * hd:(h0 + 2) * hd]
                y_ref[:, lo:lo + 2 * hd] = y_pair.astype(y_ref.dtype)
                dec = jnp.where(low_lanes[0:1], dec_rows[h0:h0 + 1, :], dec_rows[h1:h1 + 1, :])
                st_ref[d, g, :, h0 * hd:(h0 + 2) * hd] = sp * dec + _dot(jnp.concatenate([w0, w1], axis=1), x_rhs)


def _ssd_scan(act, small, ssd_prm, batch, seq):
    t = act.shape[0]
    q = SSD_CHUNK
    nc = seq // q
    gps = SSD_N_GROUPS
    xw = gps * SSD_GROUP_WIDTH
    bw = gps * SSD_D_STATE
    b0 = ACT_B // bw
    c0 = ACT_C // bw
    dtc = SMALL_DT // LANE

    def fwd(col0):
        return lambda b, k, gi: (b * nc + k, col0 + gi)

    def bwd(col0):
        return lambda b, k, gi: (b * nc + nc - 1 - k, col0 + gi)

    blocks = 2 * [((q, xw), BF16), ((q, bw), BF16), ((q, bw), BF16), ((q, LANE), F32), ((q, xw), BF16)] + [((8, LANE), F32)]
    scratch = [
        pltpu.VMEM((2, SSD_N_GROUPS, SSD_D_STATE, SSD_GROUP_WIDTH), F32),
        pltpu.VMEM((2, q, LANE), F32),
        pltpu.VMEM((4, LANE, q), F32),
    ]
    scratch_bytes = _nbytes((2, SSD_N_GROUPS, SSD_D_STATE, SSD_GROUP_WIDTH), F32) + 6 * _nbytes((q, LANE), F32)
    return pl.pallas_call(
        functools.partial(_ssd_kernel, groups_per_step=gps),
        grid=(batch, nc, SSD_N_GROUPS // gps),
        in_specs=[
            pl.BlockSpec((q, xw), fwd(0)), pl.BlockSpec((q, bw), fwd(b0)), pl.BlockSpec((q, bw), fwd(c0)),
            pl.BlockSpec((q, LANE), lambda b, k, gi: (b * nc + k, dtc)),
            pl.BlockSpec((q, xw), bwd(0)), pl.BlockSpec((q, bw), bwd(b0)), pl.BlockSpec((q, bw), bwd(c0)),
            pl.BlockSpec((q, LANE), lambda b, k, gi: (b * nc + nc - 1 - k, dtc)),
            pl.BlockSpec((8, LANE), lambda b, k, gi: (0, 0)),
        ],
        out_specs=[pl.BlockSpec((q, xw), fwd(0)), pl.BlockSpec((q, xw), bwd(0))],
        out_shape=[jax.ShapeDtypeStruct((t, SSD_D_INNER), BF16)] * 2,
        scratch_shapes=scratch,
        compiler_params=_params(("arbitrary",) * 3, blocks, scratch_bytes, temp_bytes=64 * _nbytes((q, q), F32)),
        name="ssd_scan",
    )(act, act, act, small, act, act, act, small, ssd_prm)


def _ssd_finish_kernel(yf_ref, yb_ref, xs_ref, z_ref, prm_ref, o_ref):
    z = z_ref[...].astype(F32)
    y = yf_ref[...].astype(F32) + yb_ref[...].astype(F32) + prm_ref[0:1, :] * xs_ref[...].astype(F32)
    y = y * (z * jax.nn.sigmoid(z))
    gw = SSD_GROUP_WIDTH
    for g in range(SSD_N_GROUPS):
        yg = y[:, g * gw:(g + 1) * gw]
        ms = jnp.mean(yg * yg, axis=-1, keepdims=True)
        o_ref[:, g * gw:(g + 1) * gw] = (yg * lax.rsqrt(ms + RMS_EPS) * prm_ref[1:2, g * gw:(g + 1) * gw]).astype(o_ref.dtype)


def _ssd_finish(yf, yb, act, main, fin_prm):
    t = yf.shape[0]
    ts = _tile(t, 256)
    di = SSD_D_INNER
    row_spec = pl.BlockSpec((ts, di), lambda i: (i, 0))
    blocks = 5 * [((ts, di), BF16)] + [((8, di), F32)]
    return pl.pallas_call(
        _ssd_finish_kernel,
        grid=(t // ts,),
        in_specs=[row_spec, row_spec, row_spec, row_spec, pl.BlockSpec((8, di), lambda i: (0, 0))],
        out_specs=row_spec,
        out_shape=jax.ShapeDtypeStruct((t, di), BF16),
        compiler_params=_params(("arbitrary",), blocks, temp_bytes=4 * _nbytes((ts, di), F32)),
        name="ssd_finish",
    )(yf, yb, act, main, fin_prm)


def _rope_lanes(t, tab):
    return (t * tab[:, 0:LANE] + pltpu.roll(t, LANE - MLA_ROPE_DIM // 2, 1) * tab[:, LANE:2 * LANE]
            + pltpu.roll(t, MLA_ROPE_DIM // 2, 1) * tab[:, 2 * LANE:3 * LANE])


def _rms_rows(x, w):
    return x * lax.rsqrt(jnp.mean(x * x, axis=-1, keepdims=True) + RMS_EPS) * w


def _qproj_kernel(x_ref, nw_ref, w_ref, tab_ref, qt_ref, *, scale):
    xt = x_ref[...].T
    tm = xt.shape[1]
    inv = lax.rsqrt(jnp.mean(xt * xt, axis=0, keepdims=True) + RMS_EPS)
    nw = jnp.concatenate([nw_ref[...]] * (tm // LANE), axis=1)
    xnt = (xt * inv * nw).astype(BF16)
    half = MLA_ROPE_DIM // 2
    cos = tab_ref[0:half, :]
    sin = tab_ref[half:2 * half, :]
    r0 = MLA_NOPE_DIM
    for h in range(MLA_N_HEADS):
        qt = _dot(w_ref[h * MLA_QK_PAD:(h + 1) * MLA_QK_PAD, :], xnt)
        x1 = qt[r0:r0 + half]
        x2 = qt[r0 + half:r0 + 2 * half]
        qt_ref[0, h, 0:r0, :] = (qt[0:r0] * scale).astype(qt_ref.dtype)
        qt_ref[0, h, r0:r0 + half, :] = ((x1 * cos - x2 * sin) * scale).astype(qt_ref.dtype)
        qt_ref[0, h, r0 + half:r0 + 2 * half, :] = ((x2 * cos + x1 * sin) * scale).astype(qt_ref.dtype)
        qt_ref[0, h, r0 + 2 * half:MLA_QK_PAD, :] = jnp.zeros((MLA_QK_PAD - MLA_QK_DIM, tm), qt_ref.dtype)


def _kvproj_kernel(x_ref, kr_ref, nw_ref, wk_ref, wvt_ref, tab_ref, k_ref, vt_ref, *, tk):
    cn = _rms_rows(x_ref[...], nw_ref[0:1, :])
    tm = cn.shape[0]
    k_all = _dot(cn.astype(BF16), wk_ref[...])
    vt_all = _dot(wvt_ref[...], cn.T.astype(BF16))
    k_rope = _rope_lanes(kr_ref[...], tab_ref[...]).astype(k_ref.dtype)
    pad_rows = FLASH_VT_ROWS - MLA_V_DIM
    ones_rows = (lax.broadcasted_iota(jnp.int32, (pad_rows, tk), 0) == 0).astype(vt_ref.dtype)
    for h in range(MLA_N_HEADS):
        k_ref[0, h, :, 0:LANE] = k_all[:, h * MLA_NOPE_DIM:(h + 1) * MLA_NOPE_DIM].astype(k_ref.dtype)
        k_ref[0, h, :, LANE:2 * LANE] = k_rope
        for c in range(tm // tk):
            vt_ref[0, h, c, 0:MLA_V_DIM, :] = vt_all[h * MLA_V_DIM:(h + 1) * MLA_V_DIM, c * tk:(c + 1) * tk].astype(vt_ref.dtype)
            vt_ref[0, h, c, MLA_V_DIM:FLASH_VT_ROWS, :] = ones_rows


def _mla_proj(small, q_nw_lanes, wq_t, kv_nw, wk, wv_t, rope_tab, rope_tab_t, batch, seq):
    t = small.shape[0]
    tk = _tile(seq, FLASH_TK)
    tm = _tile(seq, max(512, tk))
    nsb = seq // tm
    nh = MLA_N_HEADS
    qw = nh * MLA_QK_PAD
    kw = nh * MLA_NOPE_DIM
    vw = nh * MLA_V_DIM
    q_blocks = [((tm, MLA_Q_RANK), F32), ((MLA_Q_RANK, LANE), F32), ((qw, MLA_Q_RANK), BF16), ((MLA_ROPE_DIM, tm), F32),
                ((nh, MLA_QK_PAD, tm), BF16)]
    qt = pl.pallas_call(
        functools.partial(_qproj_kernel, scale=MLA_QK_DIM ** -0.5 * LOG2_E),
        grid=(t // tm,),
        in_specs=[pl.BlockSpec((tm, MLA_Q_RANK), lambda i: (i, SMALL_Q // MLA_Q_RANK)),
                  pl.BlockSpec((MLA_Q_RANK, LANE), lambda i: (0, 0)),
                  pl.BlockSpec((qw, MLA_Q_RANK), lambda i: (0, 0)),
                  pl.BlockSpec((MLA_ROPE_DIM, tm), lambda i: (0, i % nsb))],
        out_specs=pl.BlockSpec((1, nh, MLA_QK_PAD, tm), lambda i: (i // nsb, 0, 0, i % nsb)),
        out_shape=jax.ShapeDtypeStruct((batch, nh, MLA_QK_PAD, seq), BF16),
        compiler_params=_params(("arbitrary",), q_blocks, temp_bytes=6 * _nbytes((tm, MLA_Q_RANK), F32)),
        name="mla_q_proj",
    )(small, q_nw_lanes, wq_t, rope_tab_t)
    kv_blocks = [((tm, MLA_KV_RANK), F32), ((tm, LANE), F32), ((8, MLA_KV_RANK), F32), ((MLA_KV_RANK, kw), BF16),
                 ((vw, MLA_KV_RANK), BF16), ((tm, 3 * LANE), F32), ((nh, tm, MLA_QK_PAD), BF16), ((nh, tm // tk, FLASH_VT_ROWS, tk), BF16)]
    k, vt = pl.pallas_call(
        functools.partial(_kvproj_kernel, tk=tk),
        grid=(t // tm,),
        in_specs=[pl.BlockSpec((tm, MLA_KV_RANK), lambda i: (i, SMALL_CKV // MLA_KV_RANK)),
                  pl.BlockSpec((tm, LANE), lambda i: (i, SMALL_KROPE // LANE)),
                  pl.BlockSpec((8, MLA_KV_RANK), lambda i: (0, 0)),
                  pl.BlockSpec((MLA_KV_RANK, kw), lambda i: (0, 0)),
                  pl.BlockSpec((vw, MLA_KV_RANK), lambda i: (0, 0)),
                  pl.BlockSpec((tm, 3 * LANE), lambda i: (i % nsb, 0))],
        out_specs=[pl.BlockSpec((1, nh, tm, MLA_QK_PAD), lambda i: (i // nsb, 0, i % nsb, 0)),
                   pl.BlockSpec((1, nh, tm // tk, FLASH_VT_ROWS, tk), lambda i: (i // nsb, 0, i % nsb, 0, 0))],
        out_shape=[jax.ShapeDtypeStruct((batch, nh, seq, MLA_QK_PAD), BF16),
                   jax.ShapeDtypeStruct((batch, nh, seq // tk, FLASH_VT_ROWS, tk), BF16)],
        compiler_params=_params(("arbitrary",), kv_blocks, temp_bytes=2 * _nbytes((tm, kw), F32) + 4 * _nbytes((tm, MLA_KV_RANK), F32)),
        name="mla_kv_proj",
    )(small, small, kv_nw, wk, wv_t, rope_tab)
    return qt, k, vt


def _flash_kernel(qt_ref, k_ref, vt_ref, o_ref, s0_ref, s1_ref, p0_ref, p1_ref, acc_ref):
    tq = qt_ref.shape[3]
    n_kv, _, tk = vt_ref.shape[2:]

    def qk(j, s_ref):
        s = _dot(k_ref[0, 0, j * tk:(j + 1) * tk, :], qt_ref[0, 0])
        s_ref[...] = s
        return jnp.max(s, axis=0, keepdims=True)

    def soft(m, cm, s_ref, p_ref):
        m_new = jnp.maximum(m, cm)
        p_ref[...] = jnp.exp2(s_ref[...] - m_new).astype(p_ref.dtype)
        return m_new, jnp.exp2(m - m_new)

    def pv(j, alpha, p_ref):
        acc_ref[...] = alpha * acc_ref[...] + _dot(vt_ref[0, 0, j], p_ref[...])

    def step(j, m, alpha_prev, cm, s_cur, s_nxt, p_prev, p_cur):
        cm_next = qk(j + 1, s_nxt)
        pv(j - 1, alpha_prev, p_prev)
        m, alpha = soft(m, cm, s_cur, p_cur)
        return m, alpha, cm_next

    acc_ref[...] = jnp.zeros(acc_ref.shape, acc_ref.dtype)
    m = jnp.full((1, tq), -jnp.inf, F32)
    m, alpha = soft(m, qk(0, s0_ref), s0_ref, p0_ref)
    cm = qk(1, s1_ref)
    for j in range(1, n_kv - 1, 2):
        m, alpha, cm = step(j, m, alpha, cm, s1_ref, s0_ref, p0_ref, p1_ref)
        m, alpha, cm = step(j + 1, m, alpha, cm, s0_ref, s1_ref, p1_ref, p0_ref)
    pv(n_kv - 2, alpha, p0_ref)
    m, alpha = soft(m, cm, s1_ref, p1_ref)
    pv(n_kv - 1, alpha, p1_ref)
    o_ref[...] = (acc_ref[0:MLA_V_DIM, :] / acc_ref[MLA_V_DIM:MLA_V_DIM + 1, :]).T.astype(o_ref.dtype)


def _flash_attention(qt, k, vt):
    batch, nh, seq, _ = k.shape
    n_kv, vt_rows, tk = vt.shape[2:]
    assert n_kv >= 2 and n_kv % 2 == 0, n_kv
    tq = _tile(seq, FLASH_TQ)
    nq = seq // tq
    blocks = [((MLA_QK_PAD, tq), BF16), ((seq, MLA_QK_PAD), BF16), ((n_kv, vt_rows, tk), BF16), ((tq, MLA_V_DIM), BF16)]
    scratch = [pltpu.VMEM((tk, tq), F32), pltpu.VMEM((tk, tq), F32), pltpu.VMEM((tk, tq), BF16), pltpu.VMEM((tk, tq), BF16),
               pltpu.VMEM((vt_rows, tq), F32)]
    scratch_bytes = 3 * _nbytes((tk, tq), F32) + _nbytes((vt_rows, tq), F32)
    return pl.pallas_call(
        _flash_kernel,
        grid=(batch, nh, nq),
        in_specs=[
            pl.BlockSpec((1, 1, MLA_QK_PAD, tq), lambda b, h, i: (b, h, 0, i)),
            pl.BlockSpec((1, 1, seq, MLA_QK_PAD), lambda b, h, i: (b, h, 0, 0)),
            pl.BlockSpec((1, 1, n_kv, vt_rows, tk), lambda b, h, i: (b, h, 0, 0, 0)),
        ],
        out_specs=pl.BlockSpec((tq, MLA_V_DIM), lambda b, h, i: (b * nq + i, h)),
        out_shape=jax.ShapeDtypeStruct((batch * seq, nh * MLA_V_DIM), BF16),
        scratch_shapes=scratch,
        compiler_params=_params(("arbitrary",) * 3, blocks, scratch_bytes, temp_bytes=4 * _nbytes((tk, tq), F32)),
        name="mla_flash_attention",
    )(qt, k, vt)


def _merge_kernel(a1_ref, w1_ref, a2_ref, w2_ref, g1_ref, g2_ref, o_ref):
    y1 = _dot(a1_ref[...], w1_ref[...])
    y2 = _dot(a2_ref[...], w2_ref[...])
    g1 = jax.nn.sigmoid(g1_ref[...].astype(F32))
    g2 = jax.nn.sigmoid(g2_ref[...].astype(F32))
    o_ref[...] = (g1 * y1 + g2 * y2).astype(o_ref.dtype)


def _merge(y_ssd_n, w_ssd_out, attn, w_mla_out, main):
    t = y_ssd_n.shape[0]
    tm = _tile(t, 512)
    tn = 512
    k1, k2 = SSD_D_INNER, MLA_N_HEADS * MLA_V_DIM
    blocks = [((tm, k1), BF16), ((k1, tn), BF16), ((tm, k2), BF16), ((k2, tn), BF16), ((tm, tn), BF16), ((tm, tn), BF16), ((tm, tn), BF16)]
    return pl.pallas_call(
        _merge_kernel,
        grid=(D_MODEL // tn, t // tm),
        in_specs=[
            pl.BlockSpec((tm, k1), lambda j, i: (i, 0)), pl.BlockSpec((k1, tn), lambda j, i: (0, j)),
            pl.BlockSpec((tm, k2), lambda j, i: (i, 0)), pl.BlockSpec((k2, tn), lambda j, i: (0, j)),
            pl.BlockSpec((tm, tn), lambda j, i: (i, MAIN_GSSD // tn + j)),
            pl.BlockSpec((tm, tn), lambda j, i: (i, MAIN_GMLA // tn + j)),
        ],
        out_specs=pl.BlockSpec((tm, tn), lambda j, i: (i, j)),
        out_shape=jax.ShapeDtypeStruct((t, D_MODEL), BF16),
        compiler_params=_params(("arbitrary",) * 2, blocks, temp_bytes=4 * _nbytes((tm, tn), F32)),
        name="branch_merge",
    )(y_ssd_n, w_ssd_out, attn, w_mla_out, main, main)


def _layer_norm_rows(r, prm_ref):
    mu = jnp.mean(r, axis=-1, keepdims=True)
    c = r - mu
    var = jnp.mean(c * c, axis=-1, keepdims=True)
    return c * lax.rsqrt(var + LN_EPS) * prm_ref[0:1, :] + prm_ref[1:2, :]


def _out_ln_kernel(a_ref, w_ref, h_ref, prm_ref, of_ref, ob_ref):
    y = _layer_norm_rows(DEEPNORM_ALPHA * h_ref[...] + _dot(a_ref[...], w_ref[...]), prm_ref)
    of_ref[...] = y
    ob_ref[...] = y.astype(ob_ref.dtype)


def _out_ln(merged, w_out, h, ln_prm):
    t = h.shape[0]
    tm = _tile(t, 256)
    d = D_MODEL
    blocks = [((tm, d), BF16), ((d, d), BF16), ((tm, d), F32), ((8, d), F32), ((tm, d), F32), ((tm, d), BF16)]
    row = lambda i: (i, 0)
    return pl.pallas_call(
        _out_ln_kernel,
        grid=(t // tm,),
        in_specs=[pl.BlockSpec((tm, d), row), pl.BlockSpec((d, d), lambda i: (0, 0)), pl.BlockSpec((tm, d), row),
                  pl.BlockSpec((8, d), lambda i: (0, 0))],
        out_specs=[pl.BlockSpec((tm, d), row), pl.BlockSpec((tm, d), row)],
        out_shape=[jax.ShapeDtypeStruct((t, d), F32), jax.ShapeDtypeStruct((t, d), BF16)],
        compiler_params=_params(("arbitrary",), blocks, temp_bytes=3 * _nbytes((tm, d), F32)),
        name="mixer_out_ln",
    )(merged, w_out, h, ln_prm)


def _ffn_up_kernel(a_ref, wg_ref, wu_ref, o_ref):
    a = a_ref[...]
    gate = _dot(a, wg_ref[...])
    o_ref[...] = (gate * jax.nn.sigmoid(gate) * _dot(a, wu_ref[...])).astype(o_ref.dtype)


def _ffn_up(hb, wg, wu):
    t = hb.shape[0]
    tm = _tile(t, 1024)
    tn = 512
    d = D_MODEL
    blocks = [((tm, d), BF16), ((d, tn), BF16), ((d, tn), BF16), ((tm, tn), BF16)]
    return pl.pallas_call(
        _ffn_up_kernel,
        grid=(FFN_HIDDEN // tn, t // tm),
        in_specs=[pl.BlockSpec((tm, d), lambda j, i: (i, 0)), pl.BlockSpec((d, tn), lambda j, i: (0, j)),
                  pl.BlockSpec((d, tn), lambda j, i: (0, j))],
        out_specs=pl.BlockSpec((tm, tn), lambda j, i: (i, j)),
        out_shape=jax.ShapeDtypeStruct((t, FFN_HIDDEN), BF16),
        compiler_params=_params(("arbitrary",) * 2, blocks, temp_bytes=3 * _nbytes((tm, tn), F32)),
        name="ffn_up",
    )(hb, wg, wu)


def _ffn_down_ln_kernel(a_ref, w_ref, h_ref, prm_ref, of_ref, ob_ref, acc_ref):
    kk = pl.program_id(1)

    @pl.when(kk == 0)
    def _init():
        acc_ref[...] = DEEPNORM_ALPHA * h_ref[...]

    acc_ref[...] += _dot(a_ref[...], w_ref[...])

    @pl.when(kk == pl.num_programs(1) - 1)
    def _finish():
        y = _layer_norm_rows(acc_ref[...], prm_ref)
        of_ref[...] = y
        ob_ref[...] = y.astype(ob_ref.dtype)


def _ffn_down_ln(act, w_down, h, ln_prm):
    t = h.shape[0]
    tm = _tile(t, 512)
    tk = FFN_HIDDEN // 4
    d = D_MODEL
    blocks = [((tm, tk), BF16), ((tk, d), BF16), ((tm, d), F32), ((8, d), F32), ((tm, d), F32), ((tm, d), BF16)]
    row = lambda i, kk: (i, 0)
    return pl.pallas_call(
        _ffn_down_ln_kernel,
        grid=(t // tm, FFN_HIDDEN // tk),
        in_specs=[pl.BlockSpec((tm, tk), lambda i, kk: (i, kk)), pl.BlockSpec((tk, d), lambda i, kk: (kk, 0)),
                  pl.BlockSpec((tm, d), row), pl.BlockSpec((8, d), lambda i, kk: (0, 0))],
        out_specs=[pl.BlockSpec((tm, d), row), pl.BlockSpec((tm, d), row)],
        out_shape=[jax.ShapeDtypeStruct((t, d), F32), jax.ShapeDtypeStruct((t, d), BF16)],
        scratch_shapes=[pltpu.VMEM((tm, d), F32)],
        compiler_params=_params(("arbitrary",) * 2, blocks, _nbytes((tm, d), F32), temp_bytes=2 * _nbytes((tm, d), F32)),
        name="ffn_down_ln",
    )(act, w_down, h, ln_prm)


def _rows8(*rows):
    n = rows[0].shape[0]
    pad = jnp.zeros((8 - len(rows), n), F32)
    return jnp.concatenate([jnp.stack([r.astype(F32) for r in rows]), pad], axis=0)


def _rope_table(seq):
    pos = jnp.arange(seq, dtype=F32)
    inv_freq = ROPE_THETA ** (-jnp.arange(0, MLA_ROPE_DIM, 2, dtype=F32) / MLA_ROPE_DIM)
    ang = pos[:, None] * inv_freq[None, :]
    cos, sin = jnp.cos(ang), jnp.sin(ang)
    half = MLA_ROPE_DIM // 2
    z = lambda n: jnp.zeros((seq, n), F32)
    lanes = jnp.concatenate([cos, cos, z(LANE - 2 * half), -sin, z(LANE - half), z(half), sin, z(LANE - 2 * half)], axis=1)
    return lanes, jnp.concatenate([cos.T, sin.T], axis=0)


def _q_weight_t(w_uq):
    r = w_uq.shape[0]
    w = w_uq.reshape(r, MLA_N_HEADS, MLA_QK_DIM)
    w = jnp.pad(w, ((0, 0), (0, 0), (0, MLA_QK_PAD - MLA_QK_DIM)))
    return w.reshape(r, MLA_N_HEADS * MLA_QK_PAD).T.astype(BF16)


def _kv_weights(w_ukv):
    r = w_ukv.shape[0]
    w = w_ukv.reshape(r, MLA_N_HEADS, MLA_NOPE_DIM + MLA_V_DIM)
    wk = w[:, :, :MLA_NOPE_DIM].reshape(r, MLA_N_HEADS * MLA_NOPE_DIM)
    wv = w[:, :, MLA_NOPE_DIM:].reshape(r, MLA_N_HEADS * MLA_V_DIM)
    return wk.astype(BF16), wv.T.astype(BF16)


def kernel(x, w_in, conv_w, conv_b, ssd_a_log, ssd_dt_bias, ssd_d, ssd_norm_w, w_ssd_out, mla_q_norm_w, w_uq,
           mla_kv_norm_w, w_ukv, w_mla_out, w_out, ln1_g, ln1_b, w_ffn_gate, w_ffn_up, w_ffn_down, ln2_g, ln2_b):
    batch, seq, d = x.shape
    assert d == D_MODEL and seq % SSD_CHUNK == 0
    t = batch * seq
    hf = x.reshape(t, d)
    hb = hf.astype(BF16)
    rope_tab, rope_tab_t = _rope_table(seq)
    dt0 = MAIN_GSSD + 2 * SSD_N_HEADS
    q0 = dt0
    kv0 = q0 + MLA_Q_RANK
    kr0 = kv0 + MLA_KV_RANK
    g0 = kr0 + MLA_ROPE_DIM
    for l in range(w_in.shape[0]):
        wi = w_in[l]
        w_main = jnp.concatenate([wi[:, :MAIN_GSSD], wi[:, g0:]], axis=1).astype(BF16)
        w_small = jnp.concatenate(
            [wi[:, q0:kv0], wi[:, kv0:kr0], wi[:, MAIN_GSSD:dt0], wi[:, kr0:g0], jnp.zeros((d, LANE - MLA_ROPE_DIM), wi.dtype)],
            axis=1).astype(BF16)
        main = _matmul(hb, w_main, BF16, 1024, "in_proj_main")
        small = _matmul(hb, w_small, F32, SMALL_WIDTH, "in_proj_small")

        act = _conv_silu(main, _rows8(*[conv_w[l, i] for i in range(SSD_CONV_WIDTH)], conv_b[l]), batch, seq)
        yf, yb = _ssd_scan(act, small, _rows8(ssd_dt_bias[l].reshape(-1), ssd_a_log[l].reshape(-1)), batch, seq)
        y_ssd_n = _ssd_finish(yf, yb, act, main, _rows8(jnp.repeat(ssd_d[l], SSD_HEAD_DIM), ssd_norm_w[l]))

        wk, wv_t = _kv_weights(w_ukv[l])
        q_nw_lanes = jnp.broadcast_to(mla_q_norm_w[l].astype(F32)[:, None], (MLA_Q_RANK, LANE))
        qt, k, vt = _mla_proj(small, q_nw_lanes, _q_weight_t(w_uq[l]), _rows8(mla_kv_norm_w[l]), wk, wv_t,
                              rope_tab, rope_tab_t, batch, seq)
        attn = _flash_attention(qt, k, vt)

        merged = _merge(y_ssd_n, w_ssd_out[l].astype(BF16), attn, w_mla_out[l].astype(BF16), main)
        hf, hb = _out_ln(merged, w_out[l].astype(BF16), hf, _rows8(ln1_g[l], ln1_b[l]))
        ffn_act = _ffn_up(hb, w_ffn_gate[l].astype(BF16), w_ffn_up[l].astype(BF16))
        hf, hb = _ffn_down_ln(ffn_act, w_ffn_down[l].astype(BF16), hf, _rows8(ln2_g[l], ln2_b[l]))
    return hf.reshape(batch, seq, d)
```

```python
import functools

import jax
import jax.numpy as jnp
from jax import lax
from jax.experimental import pallas as pl
from jax.experimental.pallas import tpu as pltpu

F32 = jnp.float32
BF16 = jnp.bfloat16

D_MODEL = 2048
DEPTH = 4
SSD_D_INNER = 4096
SSD_HEAD_DIM = 64
SSD_N_HEADS = 64
SSD_N_GROUPS = 8
SSD_HEADS_PER_GROUP = 8
SSD_D_STATE = 128
SSD_CONV_WIDTH = 5
SSD_CHUNK = 128
SSD_GROUP_WIDTH = SSD_HEADS_PER_GROUP * SSD_HEAD_DIM
SSD_BC_WIDTH = SSD_N_GROUPS * SSD_D_STATE
SSD_CONV_DIM = SSD_D_INNER + 2 * SSD_BC_WIDTH
MLA_N_HEADS = 16
MLA_Q_RANK = 512
MLA_KV_RANK = 512
MLA_NOPE_DIM = 128
MLA_ROPE_DIM = 64
MLA_V_DIM = 128
MLA_QK_DIM = MLA_NOPE_DIM + MLA_ROPE_DIM
MLA_QK_PAD = 256
ROPE_THETA = 10000.0
FLASH_TQ = 512
FLASH_TK = 256
FLASH_VT_ROWS = MLA_V_DIM + 16
LOG2_E = 1.4426950408889634
FFN_HIDDEN = 5632
DEEPNORM_ALPHA = (2 * DEPTH) ** 0.25
RMS_EPS = 1e-6
LN_EPS = 1e-5

MAIN_Z = 0
MAIN_XBC = SSD_D_INNER
MAIN_GSSD = MAIN_XBC + SSD_CONV_DIM
MAIN_GMLA = MAIN_GSSD + D_MODEL
MAIN_WIDTH = MAIN_GMLA + D_MODEL
SMALL_Q = 0
SMALL_CKV = MLA_Q_RANK
SMALL_DT = SMALL_CKV + MLA_KV_RANK
SMALL_KROPE = SMALL_DT + 2 * SSD_N_HEADS
SMALL_WIDTH = SMALL_KROPE + 128
ACT_B = SSD_D_INNER
ACT_C = SSD_D_INNER + SSD_BC_WIDTH

V7X_VMEM_BYTES = 64 * 1024 * 1024
VMEM_LIMIT_CAP = 56 * 1024 * 1024
VMEM_LIMIT_FLOOR = 32 * 1024 * 1024
LANE = 128
BF16_SUBLANE_TILE = 16


def _nbytes(shape, dtype):
    n = 1
    for s in shape:
        n *= s
    return n * jnp.dtype(dtype).itemsize


def _params(semantics, blocks, scratch_bytes=0, temp_bytes=0):
    need = 2 * sum(_nbytes(s, d) for s, d in blocks) + scratch_bytes + temp_bytes
    limit = min(max(need + need // 4, VMEM_LIMIT_FLOOR), VMEM_LIMIT_CAP)
    return pltpu.CompilerParams(dimension_semantics=semantics, vmem_limit_bytes=limit)


def _tile(n, pref):
    t = min(n, pref)
    assert n % t == 0, (n, pref)
    return t


def _dot(a, b):
    return jnp.dot(a, b, preferred_element_type=F32)


def _dot_nt(a, b):
    return lax.dot_general(a, b, (((1,), (1,)), ((), ())), preferred_element_type=F32)


def _mm_kernel(a_ref, w_ref, o_ref):
    o_ref[...] = _dot(a_ref[...], w_ref[...]).astype(o_ref.dtype)


def _matmul(a, w, out_dtype, tn, name):
    m, k = a.shape
    n = w.shape[1]
    tm = _tile(m, 2048)
    tn = _tile(n, tn)
    blocks = [((tm, k), a.dtype), ((k, tn), w.dtype), ((tm, tn), out_dtype)]
    return pl.pallas_call(
        _mm_kernel,
        grid=(n // tn, m // tm),
        in_specs=[pl.BlockSpec((tm, k), lambda j, i: (i, 0)), pl.BlockSpec((k, tn), lambda j, i: (0, j))],
        out_specs=pl.BlockSpec((tm, tn), lambda j, i: (i, j)),
        out_shape=jax.ShapeDtypeStruct((m, n), out_dtype),
        compiler_params=_params(("arbitrary", "arbitrary"), blocks, temp_bytes=_nbytes((tm, tn), F32)),
        name=name,
    )(a, w)


def _conv_kernel(prev_ref, cur_ref, next_ref, w_ref, o_ref, *, n_seq_blocks):
    j = pl.program_id(1)
    cur = cur_ref[...].astype(F32)
    ts = cur.shape[0]
    has_prev = j > 0
    has_next = j < n_seq_blocks - 1
    halo = BF16_SUBLANE_TILE
    p0 = jnp.where(has_prev, prev_ref[halo - 2:halo - 1, :].astype(F32), 0.0)
    p1 = jnp.where(has_prev, prev_ref[halo - 1:halo, :].astype(F32), 0.0)
    n0 = jnp.where(has_next, next_ref[0:1, :].astype(F32), 0.0)
    n1 = jnp.where(has_next, next_ref[1:2, :].astype(F32), 0.0)
    row = lax.broadcasted_iota(jnp.int32, cur.shape, 0)
    xm1 = jnp.where(row == 0, p1, pltpu.roll(cur, 1, 0))
    xm2 = jnp.where(row == 0, p0, jnp.where(row == 1, p1, pltpu.roll(cur, 2, 0)))
    xp1 = jnp.where(row == ts - 1, n0, pltpu.roll(cur, ts - 1, 0))
    xp2 = jnp.where(row == ts - 2, n0, jnp.where(row == ts - 1, n1, pltpu.roll(cur, ts - 2, 0)))
    w = w_ref[...]
    y = xm2 * w[0:1] + xm1 * w[1:2] + cur * w[2:3] + xp1 * w[3:4] + xp2 * w[4:5] + w[5:6]
    o_ref[...] = (y * jax.nn.sigmoid(y)).astype(o_ref.dtype)


def _conv_silu(main, conv_prm, batch, seq):
    t = main.shape[0]
    ts = _tile(seq, 512)
    tc = 1024
    nsb = seq // ts
    col0 = MAIN_XBC // tc
    halo = BF16_SUBLANE_TILE
    rpb = ts // halo
    last_halo = t // halo - 1
    blocks = [((halo, tc), BF16), ((ts, tc), BF16), ((halo, tc), BF16), ((8, tc), F32), ((ts, tc), BF16)]
    return pl.pallas_call(
        functools.partial(_conv_kernel, n_seq_blocks=nsb),
        grid=(batch, nsb, SSD_CONV_DIM // tc),
        in_specs=[
            pl.BlockSpec((halo, tc), lambda b, j, c: (jnp.maximum((b * nsb + j) * rpb - 1, 0), col0 + c)),
            pl.BlockSpec((ts, tc), lambda b, j, c: (b * nsb + j, col0 + c)),
            pl.BlockSpec((halo, tc), lambda b, j, c: (jnp.minimum((b * nsb + j + 1) * rpb, last_halo), col0 + c)),
            pl.BlockSpec((8, tc), lambda b, j, c: (0, c)),
        ],
        out_specs=pl.BlockSpec((ts, tc), lambda b, j, c: (b * nsb + j, c)),
        out_shape=jax.ShapeDtypeStruct((t, SSD_CONV_DIM), BF16),
        compiler_params=_params(("arbitrary",) * 3, blocks, temp_bytes=8 * _nbytes((ts, tc), F32)),
        name="conv_silu",
    )(main, main, main, conv_prm)


def _ssd_kernel(xf_ref, bf_ref, cf_ref, dtf_ref, xb_ref, bb_ref, cb_ref, dtb_ref, prm_ref, dskip_ref,
                yf_ref, yb_ref, st_ref, col_ref, row_ref, *, groups_per_step):
    q = SSD_CHUNK
    k = pl.program_id(1)
    gi = pl.program_id(2)
    half = SSD_N_HEADS

    @pl.when(jnp.logical_and(k == 0, gi == 0))
    def _zero_state():
        st_ref[...] = jnp.zeros(st_ref.shape, st_ref.dtype)

    @pl.when(gi == 0)
    def _per_chunk_decay_terms():
        lane = lax.broadcasted_iota(jnp.int32, (q, 2 * half), 1)
        is_fwd = lane < half
        x = jnp.where(is_fwd, dtf_ref[...], dtb_ref[...]) + prm_ref[0:1, :]
        dt = jnp.maximum(x, 0.0) + jnp.log1p(jnp.exp(-jnp.abs(x)))
        a = -jnp.exp(prm_ref[1:2, :]) * dt
        r = lax.broadcasted_iota(jnp.int32, (q, q), 0)
        c = lax.broadcasted_iota(jnp.int32, (q, q), 1)
        tril = (r >= c).astype(BF16)
        triu = (r <= c).astype(BF16)
        a_hi = a.astype(BF16)
        a_lo = (a - a_hi.astype(F32)).astype(BF16)
        p = jnp.where(is_fwd, _dot(tril, a_hi) + _dot(tril, a_lo), _dot(triu, a_hi) + _dot(triu, a_lo))
        p_tot = jnp.where(is_fwd[0:1], p[q - 1:q, :], p[0:1, :])
        p_seg = p * LOG2_E
        col_ref[0] = p_seg
        col_ref[1] = jnp.exp(p)
        row_ref[0] = p_seg.T
        row_ref[1] = dt.T
        row_ref[2] = (dt * jnp.exp(p_tot - p)).T
        row_ref[3] = jnp.broadcast_to(jnp.exp(p_tot), (q, 2 * half)).T

    ri = lax.broadcasted_iota(jnp.int32, (q, q), 0)
    ci = lax.broadcasted_iota(jnp.int32, (q, q), 1)
    masks = (ri >= ci, ri <= ci)
    low_lanes = ci < SSD_HEAD_DIM
    dirs = ((xf_ref, bf_ref, cf_ref, yf_ref), (xb_ref, bb_ref, cb_ref, yb_ref))
    hd = SSD_HEAD_DIM
    zero_b = jnp.zeros((q, 2 * hd), BF16)
    er = lax.broadcasted_iota(jnp.int32, (2 * half, SSD_GROUP_WIDTH), 0)
    ec = lax.broadcasted_iota(jnp.int32, (2 * half, SSD_GROUP_WIDTH), 1)
    expand = tuple((er == d * half + ec // hd).astype(BF16) for d in range(2))
    for gg in range(groups_per_step):
        g = gi * groups_per_step + gg
        shift = lax.rem(2 * half - g * SSD_HEADS_PER_GROUP, 2 * half)
        p_cols = pltpu.roll(col_ref[0], shift, 1)
        ep_cols = pltpu.roll(col_ref[1], shift, 1)
        for d, (x_ref, b_ref, c_ref, y_ref) in enumerate(dirs):
            bg = b_ref[:, gg * SSD_D_STATE:(gg + 1) * SSD_D_STATE]
            cg = c_ref[:, gg * SSD_D_STATE:(gg + 1) * SSD_D_STATE]
            cb_masked = jnp.where(masks[d], _dot_nt(cg, bg), 0.0)
            bt = bg.astype(F32).T
            r0 = pl.multiple_of(d * half + g * SSD_HEADS_PER_GROUP, SSD_HEADS_PER_GROUP)
            p_rows = row_ref[0, pl.ds(r0, SSD_HEADS_PER_GROUP), :]
            dt_rows = row_ref[1, pl.ds(r0, SSD_HEADS_PER_GROUP), :]
            w_rows = row_ref[2, pl.ds(r0, SSD_HEADS_PER_GROUP), :]
            dec_rows = row_ref[3, pl.ds(r0, SSD_HEADS_PER_GROUP), :]
            ep_hi = ep_cols.astype(BF16)
            ep_lo = (ep_cols - ep_hi.astype(F32)).astype(BF16)
            inter = (_dot(ep_hi, expand[d]) + _dot(ep_lo, expand[d])) * _dot(cg, st_ref[d, g].astype(BF16))

            def head_terms(hh):
                li = d * half + hh
                seg = p_cols[:, li:li + 1] - p_rows[hh:hh + 1, :]
                l_mat = jnp.exp2(jnp.minimum(seg, 0.0)) * (cb_masked * dt_rows[hh:hh + 1, :])
                btw = bt * w_rows[hh:hh + 1, :]
                return l_mat.astype(BF16), btw.astype(BF16)

            for hp in range(SSD_HEADS_PER_GROUP // 2):
                h0, h1 = 2 * hp, 2 * hp + 1
                lo = (gg * SSD_HEADS_PER_GROUP + h0) * hd
                xp = x_ref[:, lo:lo + 2 * hd]
                x0 = jnp.where(low_lanes, xp, zero_b)
                x1 = jnp.where(low_lanes, zero_b, xp)
                sp = st_ref[d, g, :, h0 * hd:(h0 + 2) * hd]
                l0, w0 = head_terms(h0)
                l1, w1 = head_terms(h1)
                x_rhs = jnp.concatenate([x0, x1], axis=0)
                y_pair = _dot(jnp.concatenate([l0, l1], axis=1), x_rhs) + inter[:, h0 * hd:(h0 + 2) * hd]
                if d == 0:
                    y_pair = y_pair + dskip_ref[0:1, lo:lo + 2 * hd] * xp.astype(F32)
                y_ref[:, lo:lo + 2 * hd] = y_pair.astype(y_ref.dtype)
                dec = jnp.where(low_lanes[0:1], dec_rows[h0:h0 + 1, :], dec_rows[h1:h1 + 1, :])
                st_ref[d, g, :, h0 * hd:(h0 + 2) * hd] = sp * dec + _dot(jnp.concatenate([w0, w1], axis=1), x_rhs)


def _ssd_scan(act, small, ssd_prm, d_skip, batch, seq):
    t = act.shape[0]
    q = SSD_CHUNK
    nc = seq // q
    gps = SSD_N_GROUPS
    xw = gps * SSD_GROUP_WIDTH
    bw = gps * SSD_D_STATE
    b0 = ACT_B // bw
    c0 = ACT_C // bw
    dtc = SMALL_DT // LANE

    def fwd(col0):
        return lambda b, k, gi: (b * nc + k, col0 + gi)

    def bwd(col0):
        return lambda b, k, gi: (b * nc + nc - 1 - k, col0 + gi)

    blocks = 2 * [((q, xw), BF16), ((q, bw), BF16), ((q, bw), BF16), ((q, LANE), F32), ((q, xw), BF16)] + [((8, LANE), F32), ((8, xw), F32)]
    scratch = [
        pltpu.VMEM((2, SSD_N_GROUPS, SSD_D_STATE, SSD_GROUP_WIDTH), F32),
        pltpu.VMEM((2, q, LANE), F32),
        pltpu.VMEM((4, LANE, q), F32),
    ]
    scratch_bytes = _nbytes((2, SSD_N_GROUPS, SSD_D_STATE, SSD_GROUP_WIDTH), F32) + 6 * _nbytes((q, LANE), F32)
    return pl.pallas_call(
        functools.partial(_ssd_kernel, groups_per_step=gps),
        grid=(batch, nc, SSD_N_GROUPS // gps),
        in_specs=[
            pl.BlockSpec((q, xw), fwd(0)), pl.BlockSpec((q, bw), fwd(b0)), pl.BlockSpec((q, bw), fwd(c0)),
            pl.BlockSpec((q, LANE), lambda b, k, gi: (b * nc + k, dtc)),
            pl.BlockSpec((q, xw), bwd(0)), pl.BlockSpec((q, bw), bwd(b0)), pl.BlockSpec((q, bw), bwd(c0)),
            pl.BlockSpec((q, LANE), lambda b, k, gi: (b * nc + nc - 1 - k, dtc)),
            pl.BlockSpec((8, LANE), lambda b, k, gi: (0, 0)),
            pl.BlockSpec((8, xw), lambda b, k, gi: (0, gi)),
        ],
        out_specs=[pl.BlockSpec((q, xw), fwd(0)), pl.BlockSpec((q, xw), bwd(0))],
        out_shape=[jax.ShapeDtypeStruct((t, SSD_D_INNER), BF16)] * 2,
        scratch_shapes=scratch,
        compiler_params=_params(("arbitrary",) * 3, blocks, scratch_bytes, temp_bytes=64 * _nbytes((q, q), F32)),
        name="ssd_scan",
    )(act, act, act, small, act, act, act, small, ssd_prm, d_skip)


def _ssd_finish_kernel(yf_ref, yb_ref, z_ref, prm_ref, o_ref):
    z = z_ref[...].astype(F32)
    y = yf_ref[...].astype(F32) + yb_ref[...].astype(F32)
    y = y * (z * jax.nn.sigmoid(z))
    gw = SSD_GROUP_WIDTH
    for g in range(SSD_N_GROUPS):
        yg = y[:, g * gw:(g + 1) * gw]
        ms = jnp.mean(yg * yg, axis=-1, keepdims=True)
        o_ref[:, g * gw:(g + 1) * gw] = (yg * lax.rsqrt(ms + RMS_EPS) * prm_ref[0:1, g * gw:(g + 1) * gw]).astype(o_ref.dtype)


def _ssd_finish(yf, yb, main, fin_prm):
    t = yf.shape[0]
    ts = _tile(t, 256)
    di = SSD_D_INNER
    row_spec = pl.BlockSpec((ts, di), lambda i: (i, 0))
    blocks = 4 * [((ts, di), BF16)] + [((8, di), F32)]
    return pl.pallas_call(
        _ssd_finish_kernel,
        grid=(t // ts,),
        in_specs=[row_spec, row_spec, row_spec, pl.BlockSpec((8, di), lambda i: (0, 0))],
        out_specs=row_spec,
        out_shape=jax.ShapeDtypeStruct((t, di), BF16),
        compiler_params=_params(("arbitrary",), blocks, temp_bytes=4 * _nbytes((ts, di), F32)),
        name="ssd_finish",
    )(yf, yb, main, fin_prm)


def _rope_lanes(t, tab):
    return (t * tab[:, 0:LANE] + pltpu.roll(t, LANE - MLA_ROPE_DIM // 2, 1) * tab[:, LANE:2 * LANE]
            + pltpu.roll(t, MLA_ROPE_DIM // 2, 1) * tab[:, 2 * LANE:3 * LANE])


def _rms_rows(x, w):
    return x * lax.rsqrt(jnp.mean(x * x, axis=-1, keepdims=True) + RMS_EPS) * w


def _qproj_kernel(x_ref, nw_ref, w_ref, tab_ref, qt_ref, *, scale):
    xt = x_ref[...].T
    tm = xt.shape[1]
    inv = lax.rsqrt(jnp.mean(xt * xt, axis=0, keepdims=True) + RMS_EPS)
    nw = jnp.concatenate([nw_ref[...]] * (tm // LANE), axis=1)
    xnt = (xt * inv * nw).astype(BF16)
    half = MLA_ROPE_DIM // 2
    cos = tab_ref[0:half, :]
    sin = tab_ref[half:2 * half, :]
    r0 = MLA_NOPE_DIM
    for h in range(MLA_N_HEADS):
        qt = _dot(w_ref[h * MLA_QK_PAD:(h + 1) * MLA_QK_PAD, :], xnt)
        x1 = qt[r0:r0 + half]
        x2 = qt[r0 + half:r0 + 2 * half]
        qt_ref[0, h, 0:r0, :] = (qt[0:r0] * scale).astype(qt_ref.dtype)
        qt_ref[0, h, r0:r0 + half, :] = ((x1 * cos - x2 * sin) * scale).astype(qt_ref.dtype)
        qt_ref[0, h, r0 + half:r0 + 2 * half, :] = ((x2 * cos + x1 * sin) * scale).astype(qt_ref.dtype)
        qt_ref[0, h, r0 + 2 * half:MLA_QK_PAD, :] = jnp.zeros((MLA_QK_PAD - MLA_QK_DIM, tm), qt_ref.dtype)


def _kvproj_kernel(x_ref, kr_ref, nw_ref, wk_ref, wvt_ref, tab_ref, k_ref, vt_ref, *, tk):
    cn = _rms_rows(x_ref[...], nw_ref[0:1, :])
    tm = cn.shape[0]
    k_all = _dot(cn.astype(BF16), wk_ref[...])
    vt_all = _dot(wvt_ref[...], cn.T.astype(BF16))
    k_rope = _rope_lanes(kr_ref[...], tab_ref[...]).astype(k_ref.dtype)
    pad_rows = FLASH_VT_ROWS - MLA_V_DIM
    ones_rows = (lax.broadcasted_iota(jnp.int32, (pad_rows, tk), 0) == 0).astype(vt_ref.dtype)
    for h in range(MLA_N_HEADS):
        k_ref[0, h, :, 0:LANE] = k_all[:, h * MLA_NOPE_DIM:(h + 1) * MLA_NOPE_DIM].astype(k_ref.dtype)
        k_ref[0, h, :, LANE:2 * LANE] = k_rope
        for c in range(tm // tk):
            vt_ref[0, h, c, 0:MLA_V_DIM, :] = vt_all[h * MLA_V_DIM:(h + 1) * MLA_V_DIM, c * tk:(c + 1) * tk].astype(vt_ref.dtype)
            vt_ref[0, h, c, MLA_V_DIM:FLASH_VT_ROWS, :] = ones_rows


def _mla_proj(small, q_nw_lanes, wq_t, kv_nw, wk, wv_t, rope_tab, rope_tab_t, batch, seq):
    t = small.shape[0]
    tk = _tile(seq, FLASH_TK)
    tm = _tile(seq, max(512, tk))
    nsb = seq // tm
    nh = MLA_N_HEADS
    qw = nh * MLA_QK_PAD
    kw = nh * MLA_NOPE_DIM
    vw = nh * MLA_V_DIM
    q_blocks = [((tm, MLA_Q_RANK), F32), ((MLA_Q_RANK, LANE), F32), ((qw, MLA_Q_RANK), BF16), ((MLA_ROPE_DIM, tm), F32),
                ((nh, MLA_QK_PAD, tm), BF16)]
    qt = pl.pallas_call(
        functools.partial(_qproj_kernel, scale=MLA_QK_DIM ** -0.5 * LOG2_E),
        grid=(t // tm,),
        in_specs=[pl.BlockSpec((tm, MLA_Q_RANK), lambda i: (i, SMALL_Q // MLA_Q_RANK)),
                  pl.BlockSpec((MLA_Q_RANK, LANE), lambda i: (0, 0)),
                  pl.BlockSpec((qw, MLA_Q_RANK), lambda i: (0, 0)),
                  pl.BlockSpec((MLA_ROPE_DIM, tm), lambda i: (0, i % nsb))],
        out_specs=pl.BlockSpec((1, nh, MLA_QK_PAD, tm), lambda i: (i // nsb, 0, 0, i % nsb)),
        out_shape=jax.ShapeDtypeStruct((batch, nh, MLA_QK_PAD, seq), BF16),
        compiler_params=_params(("arbitrary",), q_blocks, temp_bytes=6 * _nbytes((tm, MLA_Q_RANK), F32)),
        name="mla_q_proj",
    )(small, q_nw_lanes, wq_t, rope_tab_t)
    kv_blocks = [((tm, MLA_KV_RANK), F32), ((tm, LANE), F32), ((8, MLA_KV_RANK), F32), ((MLA_KV_RANK, kw), BF16),
                 ((vw, MLA_KV_RANK), BF16), ((tm, 3 * LANE), F32), ((nh, tm, MLA_QK_PAD), BF16), ((nh, tm // tk, FLASH_VT_ROWS, tk), BF16)]
    k, vt = pl.pallas_call(
        functools.partial(_kvproj_kernel, tk=tk),
        grid=(t // tm,),
        in_specs=[pl.BlockSpec((tm, MLA_KV_RANK), lambda i: (i, SMALL_CKV // MLA_KV_RANK)),
                  pl.BlockSpec((tm, LANE), lambda i: (i, SMALL_KROPE // LANE)),
                  pl.BlockSpec((8, MLA_KV_RANK), lambda i: (0, 0)),
                  pl.BlockSpec((MLA_KV_RANK, kw), lambda i: (0, 0)),
                  pl.BlockSpec((vw, MLA_KV_RANK), lambda i: (0, 0)),
                  pl.BlockSpec((tm, 3 * LANE), lambda i: (i % nsb, 0))],
        out_specs=[pl.BlockSpec((1, nh, tm, MLA_QK_PAD), lambda i: (i // nsb, 0, i % nsb, 0)),
                   pl.BlockSpec((1, nh, tm // tk, FLASH_VT_ROWS, tk), lambda i: (i // nsb, 0, i % nsb, 0, 0))],
        out_shape=[jax.ShapeDtypeStruct((batch, nh, seq, MLA_QK_PAD), BF16),
                   jax.ShapeDtypeStruct((batch, nh, seq // tk, FLASH_VT_ROWS, tk), BF16)],
        compiler_params=_params(("arbitrary",), kv_blocks, temp_bytes=2 * _nbytes((tm, kw), F32) + 4 * _nbytes((tm, MLA_KV_RANK), F32)),
        name="mla_kv_proj",
    )(small, small, kv_nw, wk, wv_t, rope_tab)
    return qt, k, vt


def _flash_kernel(qt_ref, k_ref, vt_ref, o_ref, s0_ref, s1_ref, p0_ref, p1_ref, acc_ref):
    tq = qt_ref.shape[3]
    n_kv, _, tk = vt_ref.shape[2:]

    def qk(j, s_ref):
        s = _dot(k_ref[0, 0, j * tk:(j + 1) * tk, :], qt_ref[0, 0])
        s_ref[...] = s
        return jnp.max(s, axis=0, keepdims=True)

    def soft(m, cm, s_ref, p_ref):
        m_new = jnp.maximum(m, cm)
        p_ref[...] = jnp.exp2(s_ref[...] - m_new).astype(p_ref.dtype)
        return m_new, jnp.exp2(m - m_new)

    def pv(j, alpha, p_ref):
        acc_ref[...] = alpha * acc_ref[...] + _dot(vt_ref[0, 0, j], p_ref[...])

    def step(j, m, alpha_prev, cm, s_cur, s_nxt, p_prev, p_cur):
        cm_next = qk(j + 1, s_nxt)
        pv(j - 1, alpha_prev, p_prev)
        m, alpha = soft(m, cm, s_cur, p_cur)
        return m, alpha, cm_next

    acc_ref[...] = jnp.zeros(acc_ref.shape, acc_ref.dtype)
    m = jnp.full((1, tq), -jnp.inf, F32)
    m, alpha = soft(m, qk(0, s0_ref), s0_ref, p0_ref)
    cm = qk(1, s1_ref)
    for j in range(1, n_kv - 1, 2):
        m, alpha, cm = step(j, m, alpha, cm, s1_ref, s0_ref, p0_ref, p1_ref)
        m, alpha, cm = step(j + 1, m, alpha, cm, s0_ref, s1_ref, p1_ref, p0_ref)
    pv(n_kv - 2, alpha, p0_ref)
    m, alpha = soft(m, cm, s1_ref, p1_ref)
    pv(n_kv - 1, alpha, p1_ref)
    o_ref[...] = (acc_ref[0:MLA_V_DIM, :] / acc_ref[MLA_V_DIM:MLA_V_DIM + 1, :]).T.astype(o_ref.dtype)


def _flash_attention(qt, k, vt):
    batch, nh, seq, _ = k.shape
    n_kv, vt_rows, tk = vt.shape[2:]
    assert n_kv >= 2 and n_kv % 2 == 0, n_kv
    tq = _tile(seq, FLASH_TQ)
    nq = seq // tq
    blocks = [((MLA_QK_PAD, tq), BF16), ((seq, MLA_QK_PAD), BF16), ((n_kv, vt_rows, tk), BF16), ((tq, MLA_V_DIM), BF16)]
    scratch = [pltpu.VMEM((tk, tq), F32), pltpu.VMEM((tk, tq), F32), pltpu.VMEM((tk, tq), BF16), pltpu.VMEM((tk, tq), BF16),
               pltpu.VMEM((vt_rows, tq), F32)]
    scratch_bytes = 3 * _nbytes((tk, tq), F32) + _nbytes((vt_rows, tq), F32)
    return pl.pallas_call(
        _flash_kernel,
        grid=(batch, nh, nq),
        in_specs=[
            pl.BlockSpec((1, 1, MLA_QK_PAD, tq), lambda b, h, i: (b, h, 0, i)),
            pl.BlockSpec((1, 1, seq, MLA_QK_PAD), lambda b, h, i: (b, h, 0, 0)),
            pl.BlockSpec((1, 1, n_kv, vt_rows, tk), lambda b, h, i: (b, h, 0, 0, 0)),
        ],
        out_specs=pl.BlockSpec((tq, MLA_V_DIM), lambda b, h, i: (b * nq + i, h)),
        out_shape=jax.ShapeDtypeStruct((batch * seq, nh * MLA_V_DIM), BF16),
        scratch_shapes=scratch,
        compiler_params=_params(("arbitrary",) * 3, blocks, scratch_bytes, temp_bytes=4 * _nbytes((tk, tq), F32)),
        name="mla_flash_attention",
    )(qt, k, vt)


def _merge_kernel(a1_ref, w1_ref, a2_ref, w2_ref, g1_ref, g2_ref, o_ref):
    y1 = _dot(a1_ref[...], w1_ref[...])
    y2 = _dot(a2_ref[...], w2_ref[...])
    g1 = jax.nn.sigmoid(g1_ref[...].astype(F32))
    g2 = jax.nn.sigmoid(g2_ref[...].astype(F32))
    o_ref[...] = (g1 * y1 + g2 * y2).astype(o_ref.dtype)


def _merge(y_ssd_n, w_ssd_out, attn, w_mla_out, main):
    t = y_ssd_n.shape[0]
    tm = _tile(t, 1024)
    tn = 512
    k1, k2 = SSD_D_INNER, MLA_N_HEADS * MLA_V_DIM
    streamed = [((tm, k1), BF16), ((tm, k2), BF16), ((tm, tn), BF16), ((tm, tn), BF16), ((tm, tn), BF16)]
    resident = _nbytes((k1, tn), BF16) + _nbytes((k2, tn), BF16)
    once = pl.Buffered(1)
    return pl.pallas_call(
        _merge_kernel,
        grid=(D_MODEL // tn, t // tm),
        in_specs=[
            pl.BlockSpec((tm, k1), lambda j, i: (i, 0)), pl.BlockSpec((k1, tn), lambda j, i: (0, j), pipeline_mode=once),
            pl.BlockSpec((tm, k2), lambda j, i: (i, 0)), pl.BlockSpec((k2, tn), lambda j, i: (0, j), pipeline_mode=once),
            pl.BlockSpec((tm, tn), lambda j, i: (i, MAIN_GSSD // tn + j)),
            pl.BlockSpec((tm, tn), lambda j, i: (i, MAIN_GMLA // tn + j)),
        ],
        out_specs=pl.BlockSpec((tm, tn), lambda j, i: (i, j)),
        out_shape=jax.ShapeDtypeStruct((t, D_MODEL), BF16),
        compiler_params=_params(("arbitrary",) * 2, streamed, scratch_bytes=resident, temp_bytes=4 * _nbytes((tm, tn), F32)),
        name="branch_merge",
    )(y_ssd_n, w_ssd_out, attn, w_mla_out, main, main)


def _layer_norm_rows(r, prm_ref):
    mu = jnp.mean(r, axis=-1, keepdims=True)
    c = r - mu
    var = jnp.mean(c * c, axis=-1, keepdims=True)
    return c * lax.rsqrt(var + LN_EPS) * prm_ref[0:1, :] + prm_ref[1:2, :]


def _out_ln_kernel(a_ref, w_ref, h_ref, prm_ref, of_ref, ob_ref):
    y = _layer_norm_rows(DEEPNORM_ALPHA * h_ref[...] + _dot(a_ref[...], w_ref[...]), prm_ref)
    of_ref[...] = y
    ob_ref[...] = y.astype(ob_ref.dtype)


def _out_ln(merged, w_out, h, ln_prm):
    t = h.shape[0]
    tm = _tile(t, 512)
    d = D_MODEL
    streamed = [((tm, d), BF16), ((tm, d), F32), ((8, d), F32), ((tm, d), F32), ((tm, d), BF16)]
    row = lambda i: (i, 0)
    return pl.pallas_call(
        _out_ln_kernel,
        grid=(t // tm,),
        in_specs=[pl.BlockSpec((tm, d), row), pl.BlockSpec((d, d), lambda i: (0, 0), pipeline_mode=pl.Buffered(1)), pl.BlockSpec((tm, d), row),
                  pl.BlockSpec((8, d), lambda i: (0, 0))],
        out_specs=[pl.BlockSpec((tm, d), row), pl.BlockSpec((tm, d), row)],
        out_shape=[jax.ShapeDtypeStruct((t, d), F32), jax.ShapeDtypeStruct((t, d), BF16)],
        compiler_params=_params(("arbitrary",), streamed, scratch_bytes=_nbytes((d, d), BF16), temp_bytes=3 * _nbytes((tm, d), F32)),
        name="mixer_out_ln",
    )(merged, w_out, h, ln_prm)


def _ffn_up_kernel(a_ref, wg_ref, wu_ref, o_ref):
    a = a_ref[...]
    gate = _dot(a, wg_ref[...])
    o_ref[...] = (gate * jax.nn.sigmoid(gate) * _dot(a, wu_ref[...])).astype(o_ref.dtype)


def _ffn_up(hb, wg, wu):
    t = hb.shape[0]
    tm = _tile(t, 1024)
    tn = 512
    d = D_MODEL
    blocks = [((tm, d), BF16), ((d, tn), BF16), ((d, tn), BF16), ((tm, tn), BF16)]
    return pl.pallas_call(
        _ffn_up_kernel,
        grid=(FFN_HIDDEN // tn, t // tm),
        in_specs=[pl.BlockSpec((tm, d), lambda j, i: (i, 0)), pl.BlockSpec((d, tn), lambda j, i: (0, j)),
                  pl.BlockSpec((d, tn), lambda j, i: (0, j))],
        out_specs=pl.BlockSpec((tm, tn), lambda j, i: (i, j)),
        out_shape=jax.ShapeDtypeStruct((t, FFN_HIDDEN), BF16),
        compiler_params=_params(("arbitrary",) * 2, blocks, temp_bytes=3 * _nbytes((tm, tn), F32)),
        name="ffn_up",
    )(hb, wg, wu)


def _ffn_down_ln_kernel(a_ref, w_ref, h_ref, prm_ref, of_ref, ob_ref):
    y = _layer_norm_rows(DEEPNORM_ALPHA * h_ref[...] + _dot(a_ref[...], w_ref[...]), prm_ref)
    of_ref[...] = y
    ob_ref[...] = y.astype(ob_ref.dtype)


def _ffn_down_ln(act, w_down, h, ln_prm):
    t = h.shape[0]
    tm = _tile(t, 256)
    d = D_MODEL
    kk = FFN_HIDDEN
    row = lambda i: (i, 0)
    streamed = [((tm, kk), BF16), ((tm, d), F32), ((8, d), F32), ((tm, d), F32), ((tm, d), BF16)]
    return pl.pallas_call(
        _ffn_down_ln_kernel,
        grid=(t // tm,),
        in_specs=[pl.BlockSpec((tm, kk), row), pl.BlockSpec((kk, d), lambda i: (0, 0), pipeline_mode=pl.Buffered(1)),
                  pl.BlockSpec((tm, d), row), pl.BlockSpec((8, d), lambda i: (0, 0))],
        out_specs=[pl.BlockSpec((tm, d), row), pl.BlockSpec((tm, d), row)],
        out_shape=[jax.ShapeDtypeStruct((t, d), F32), jax.ShapeDtypeStruct((t, d), BF16)],
        compiler_params=_params(("arbitrary",), streamed, scratch_bytes=_nbytes((kk, d), BF16), temp_bytes=3 * _nbytes((tm, d), F32)),
        name="ffn_down_ln",
    )(act, w_down, h, ln_prm)


def _rows8(*rows):
    n = rows[0].shape[0]
    pad = jnp.zeros((8 - len(rows), n), F32)
    return jnp.concatenate([jnp.stack([r.astype(F32) for r in rows]), pad], axis=0)


def _rope_table(seq):
    pos = jnp.arange(seq, dtype=F32)
    inv_freq = ROPE_THETA ** (-jnp.arange(0, MLA_ROPE_DIM, 2, dtype=F32) / MLA_ROPE_DIM)
    ang = pos[:, None] * inv_freq[None, :]
    cos, sin = jnp.cos(ang), jnp.sin(ang)
    half = MLA_ROPE_DIM // 2
    z = lambda n: jnp.zeros((seq, n), F32)
    lanes = jnp.concatenate([cos, cos, z(LANE - 2 * half), -sin, z(LANE - half), z(half), sin, z(LANE - 2 * half)], axis=1)
    return lanes, jnp.concatenate([cos.T, sin.T], axis=0)


def _q_weight_t(w_uq):
    r = w_uq.shape[0]
    w = w_uq.reshape(r, MLA_N_HEADS, MLA_QK_DIM)
    w = jnp.pad(w, ((0, 0), (0, 0), (0, MLA_QK_PAD - MLA_QK_DIM)))
    return w.reshape(r, MLA_N_HEADS * MLA_QK_PAD).T.astype(BF16)


def _kv_weights(w_ukv):
    r = w_ukv.shape[0]
    w = w_ukv.reshape(r, MLA_N_HEADS, MLA_NOPE_DIM + MLA_V_DIM)
    wk = w[:, :, :MLA_NOPE_DIM].reshape(r, MLA_N_HEADS * MLA_NOPE_DIM)
    wv = w[:, :, MLA_NOPE_DIM:].reshape(r, MLA_N_HEADS * MLA_V_DIM)
    return wk.astype(BF16), wv.T.astype(BF16)


def kernel(x, w_in, conv_w, conv_b, ssd_a_log, ssd_dt_bias, ssd_d, ssd_norm_w, w_ssd_out, mla_q_norm_w, w_uq,
           mla_kv_norm_w, w_ukv, w_mla_out, w_out, ln1_g, ln1_b, w_ffn_gate, w_ffn_up, w_ffn_down, ln2_g, ln2_b):
    batch, seq, d = x.shape
    assert d == D_MODEL and seq % SSD_CHUNK == 0
    t = batch * seq
    hf = x.reshape(t, d)
    hb = hf.astype(BF16)
    rope_tab, rope_tab_t = _rope_table(seq)
    dt0 = MAIN_GSSD + 2 * SSD_N_HEADS
    q0 = dt0
    kv0 = q0 + MLA_Q_RANK
    kr0 = kv0 + MLA_KV_RANK
    g0 = kr0 + MLA_ROPE_DIM
    for l in range(w_in.shape[0]):
        wi = w_in[l]
        w_main = jnp.concatenate([wi[:, :MAIN_GSSD], wi[:, g0:]], axis=1).astype(BF16)
        w_small = jnp.concatenate(
            [wi[:, q0:kv0], wi[:, kv0:kr0], wi[:, MAIN_GSSD:dt0], wi[:, kr0:g0], jnp.zeros((d, LANE - MLA_ROPE_DIM), wi.dtype)],
            axis=1).astype(BF16)
        main = _matmul(hb, w_main, BF16, 1024, "in_proj_main")
        small = _matmul(hb, w_small, F32, SMALL_WIDTH, "in_proj_small")

        act = _conv_silu(main, _rows8(*[conv_w[l, i] for i in range(SSD_CONV_WIDTH)], conv_b[l]), batch, seq)
        yf, yb = _ssd_scan(act, small, _rows8(ssd_dt_bias[l].reshape(-1), ssd_a_log[l].reshape(-1)),
                           _rows8(jnp.repeat(ssd_d[l], SSD_HEAD_DIM)), batch, seq)
        y_ssd_n = _ssd_finish(yf, yb, main, _rows8(ssd_norm_w[l]))

        wk, wv_t = _kv_weights(w_ukv[l])
        q_nw_lanes = jnp.broadcast_to(mla_q_norm_w[l].astype(F32)[:, None], (MLA_Q_RANK, LANE))
        qt, k, vt = _mla_proj(small, q_nw_lanes, _q_weight_t(w_uq[l]), _rows8(mla_kv_norm_w[l]), wk, wv_t,
                              rope_tab, rope_tab_t, batch, seq)
        attn = _flash_attention(qt, k, vt)

        merged = _merge(y_ssd_n, w_ssd_out[l].astype(BF16), attn, w_mla_out[l].astype(BF16), main)
        hf, hb = _out_ln(merged, w_out[l].astype(BF16), hf, _rows8(ln1_g[l], ln1_b[l]))
        ffn_act = _ffn_up(hb, w_ffn_gate[l].astype(BF16), w_ffn_up[l].astype(BF16))
        hf, hb = _ffn_down_ln(ffn_act, w_ffn_down[l].astype(BF16), hf, _rows8(ln2_g[l], ln2_b[l]))
    return hf.reshape(batch, seq, d)
```

```python
import functools

import jax
import jax.numpy as jnp
from jax import lax
from jax.experimental import pallas as pl
from jax.experimental.pallas import tpu as pltpu

F32 = jnp.float32
BF16 = jnp.bfloat16

D_MODEL = 2048
DEPTH = 4
SSD_D_INNER = 4096
SSD_HEAD_DIM = 64
SSD_N_HEADS = 64
SSD_N_GROUPS = 8
SSD_HEADS_PER_GROUP = 8
SSD_D_STATE = 128
SSD_CONV_WIDTH = 5
SSD_CHUNK = 128
SSD_GROUP_WIDTH = SSD_HEADS_PER_GROUP * SSD_HEAD_DIM
SSD_BC_WIDTH = SSD_N_GROUPS * SSD_D_STATE
SSD_CONV_DIM = SSD_D_INNER + 2 * SSD_BC_WIDTH
MLA_N_HEADS = 16
MLA_Q_RANK = 512
MLA_KV_RANK = 512
MLA_NOPE_DIM = 128
MLA_ROPE_DIM = 64
MLA_V_DIM = 128
MLA_QK_DIM = MLA_NOPE_DIM + MLA_ROPE_DIM
MLA_QK_PAD = 256
ROPE_THETA = 10000.0
FLASH_TQ = 512
FLASH_TK = 256
FLASH_VT_ROWS = MLA_V_DIM + 16
LOG2_E = 1.4426950408889634
FFN_HIDDEN = 5632
DEEPNORM_ALPHA = (2 * DEPTH) ** 0.25
RMS_EPS = 1e-6
LN_EPS = 1e-5

MAIN_Z = 0
MAIN_XBC = SSD_D_INNER
MAIN_GSSD = MAIN_XBC + SSD_CONV_DIM
MAIN_GMLA = MAIN_GSSD + D_MODEL
MAIN_WIDTH = MAIN_GMLA + D_MODEL
SMALL_Q = 0
SMALL_CKV = MLA_Q_RANK
SMALL_DT = SMALL_CKV + MLA_KV_RANK
SMALL_KROPE = SMALL_DT + 2 * SSD_N_HEADS
SMALL_WIDTH = SMALL_KROPE + 128
ACT_B = SSD_D_INNER
ACT_C = SSD_D_INNER + SSD_BC_WIDTH

V7X_VMEM_BYTES = 64 * 1024 * 1024
VMEM_LIMIT_CAP = 56 * 1024 * 1024
VMEM_LIMIT_FLOOR = 32 * 1024 * 1024
LANE = 128
BF16_SUBLANE_TILE = 16


def _nbytes(shape, dtype):
    n = 1
    for s in shape:
        n *= s
    return n * jnp.dtype(dtype).itemsize


def _params(semantics, blocks, scratch_bytes=0, temp_bytes=0):
    need = 2 * sum(_nbytes(s, d) for s, d in blocks) + scratch_bytes + temp_bytes
    limit = min(max(need + need // 4, VMEM_LIMIT_FLOOR), VMEM_LIMIT_CAP)
    return pltpu.CompilerParams(dimension_semantics=semantics, vmem_limit_bytes=limit)


def _tile(n, pref):
    t = min(n, pref)
    assert n % t == 0, (n, pref)
    return t


def _dot(a, b):
    return jnp.dot(a, b, preferred_element_type=F32)


def _dot_nt(a, b):
    return lax.dot_general(a, b, (((1,), (1,)), ((), ())), preferred_element_type=F32)


def _mm_kernel(a_ref, w_ref, o_ref):
    o_ref[...] = _dot(a_ref[...].astype(BF16), w_ref[...]).astype(o_ref.dtype)


def _matmul(a, w, out_dtype, tn, name):
    m, k = a.shape
    n = w.shape[1]
    tm = _tile(m, 2048 if a.dtype == BF16 else 1024)
    tn = _tile(n, tn)
    blocks = [((tm, k), a.dtype), ((k, tn), w.dtype), ((tm, tn), out_dtype)]
    return pl.pallas_call(
        _mm_kernel,
        grid=(n // tn, m // tm),
        in_specs=[pl.BlockSpec((tm, k), lambda j, i: (i, 0)), pl.BlockSpec((k, tn), lambda j, i: (0, j))],
        out_specs=pl.BlockSpec((tm, tn), lambda j, i: (i, j)),
        out_shape=jax.ShapeDtypeStruct((m, n), out_dtype),
        compiler_params=_params(("arbitrary", "arbitrary"), blocks, temp_bytes=_nbytes((tm, tn), F32)),
        name=name,
    )(a, w)


def _conv_kernel(prev_ref, cur_ref, next_ref, w_ref, o_ref, *, n_seq_blocks):
    j = pl.program_id(1)
    cur = cur_ref[...].astype(F32)
    ts = cur.shape[0]
    has_prev = j > 0
    has_next = j < n_seq_blocks - 1
    halo = BF16_SUBLANE_TILE
    p0 = jnp.where(has_prev, prev_ref[halo - 2:halo - 1, :].astype(F32), 0.0)
    p1 = jnp.where(has_prev, prev_ref[halo - 1:halo, :].astype(F32), 0.0)
    n0 = jnp.where(has_next, next_ref[0:1, :].astype(F32), 0.0)
    n1 = jnp.where(has_next, next_ref[1:2, :].astype(F32), 0.0)
    row = lax.broadcasted_iota(jnp.int32, cur.shape, 0)
    xm1 = jnp.where(row == 0, p1, pltpu.roll(cur, 1, 0))
    xm2 = jnp.where(row == 0, p0, jnp.where(row == 1, p1, pltpu.roll(cur, 2, 0)))
    xp1 = jnp.where(row == ts - 1, n0, pltpu.roll(cur, ts - 1, 0))
    xp2 = jnp.where(row == ts - 2, n0, jnp.where(row == ts - 1, n1, pltpu.roll(cur, ts - 2, 0)))
    w = w_ref[...]
    y = xm2 * w[0:1] + xm1 * w[1:2] + cur * w[2:3] + xp1 * w[3:4] + xp2 * w[4:5] + w[5:6]
    o_ref[...] = (y * jax.nn.sigmoid(y)).astype(o_ref.dtype)


def _conv_silu(main, conv_prm, batch, seq):
    t = main.shape[0]
    ts = _tile(seq, 512)
    tc = 1024
    nsb = seq // ts
    col0 = MAIN_XBC // tc
    halo = BF16_SUBLANE_TILE
    rpb = ts // halo
    last_halo = t // halo - 1
    blocks = [((halo, tc), BF16), ((ts, tc), BF16), ((halo, tc), BF16), ((8, tc), F32), ((ts, tc), BF16)]
    return pl.pallas_call(
        functools.partial(_conv_kernel, n_seq_blocks=nsb),
        grid=(batch, nsb, SSD_CONV_DIM // tc),
        in_specs=[
            pl.BlockSpec((halo, tc), lambda b, j, c: (jnp.maximum((b * nsb + j) * rpb - 1, 0), col0 + c)),
            pl.BlockSpec((ts, tc), lambda b, j, c: (b * nsb + j, col0 + c)),
            pl.BlockSpec((halo, tc), lambda b, j, c: (jnp.minimum((b * nsb + j + 1) * rpb, last_halo), col0 + c)),
            pl.BlockSpec((8, tc), lambda b, j, c: (0, c)),
        ],
        out_specs=pl.BlockSpec((ts, tc), lambda b, j, c: (b * nsb + j, c)),
        out_shape=jax.ShapeDtypeStruct((t, SSD_CONV_DIM), BF16),
        compiler_params=_params(("arbitrary",) * 3, blocks, temp_bytes=8 * _nbytes((ts, tc), F32)),
        name="conv_silu",
    )(main, main, main, conv_prm)


def _ssd_kernel(xf_ref, bf_ref, cf_ref, dtf_ref, xb_ref, bb_ref, cb_ref, dtb_ref, prm_ref, dskip_ref,
                yf_ref, yb_ref, st_ref, col_ref, row_ref, *, groups_per_step):
    q = SSD_CHUNK
    k = pl.program_id(1)
    gi = pl.program_id(2)
    half = SSD_N_HEADS

    @pl.when(jnp.logical_and(k == 0, gi == 0))
    def _zero_state():
        st_ref[...] = jnp.zeros(st_ref.shape, st_ref.dtype)

    @pl.when(gi == 0)
    def _per_chunk_decay_terms():
        lane = lax.broadcasted_iota(jnp.int32, (q, 2 * half), 1)
        is_fwd = lane < half
        x = jnp.where(is_fwd, dtf_ref[...], dtb_ref[...]) + prm_ref[0:1, :]
        dt = jnp.maximum(x, 0.0) + jnp.log1p(jnp.exp(-jnp.abs(x)))
        a = -jnp.exp(prm_ref[1:2, :]) * dt
        r = lax.broadcasted_iota(jnp.int32, (q, q), 0)
        c = lax.broadcasted_iota(jnp.int32, (q, q), 1)
        tril = (r >= c).astype(BF16)
        triu = (r <= c).astype(BF16)
        a_hi = a.astype(BF16)
        a_lo = (a - a_hi.astype(F32)).astype(BF16)
        p = jnp.where(is_fwd, _dot(tril, a_hi) + _dot(tril, a_lo), _dot(triu, a_hi) + _dot(triu, a_lo))
        p_tot = jnp.where(is_fwd[0:1], p[q - 1:q, :], p[0:1, :])
        p_seg = p * LOG2_E
        col_ref[0] = p_seg
        col_ref[1] = jnp.exp(p)
        row_ref[0] = p_seg.T
        row_ref[1] = dt.T
        row_ref[2] = (dt * jnp.exp(p_tot - p)).T
        row_ref[3] = jnp.broadcast_to(jnp.exp(p_tot), (q, 2 * half)).T

    ri = lax.broadcasted_iota(jnp.int32, (q, q), 0)
    ci = lax.broadcasted_iota(jnp.int32, (q, q), 1)
    masks = (ri >= ci, ri <= ci)
    low_lanes = ci < SSD_HEAD_DIM
    dirs = ((xf_ref, bf_ref, cf_ref, yf_ref), (xb_ref, bb_ref, cb_ref, yb_ref))
    hd = SSD_HEAD_DIM
    zero_b = jnp.zeros((q, 2 * hd), BF16)
    er = lax.broadcasted_iota(jnp.int32, (2 * half, SSD_GROUP_WIDTH), 0)
    ec = lax.broadcasted_iota(jnp.int32, (2 * half, SSD_GROUP_WIDTH), 1)
    expand = tuple((er == d * half + ec // hd).astype(BF16) for d in range(2))
    for gg in range(groups_per_step):
        g = gi * groups_per_step + gg
        shift = lax.rem(2 * half - g * SSD_HEADS_PER_GROUP, 2 * half)
        p_cols = pltpu.roll(col_ref[0], shift, 1)
        ep_cols = pltpu.roll(col_ref[1], shift, 1)
        for d, (x_ref, b_ref, c_ref, y_ref) in enumerate(dirs):
            bg = b_ref[:, gg * SSD_D_STATE:(gg + 1) * SSD_D_STATE]
            cg = c_ref[:, gg * SSD_D_STATE:(gg + 1) * SSD_D_STATE]
            cb_masked = jnp.where(masks[d], _dot_nt(cg, bg), 0.0)
            bt = bg.astype(F32).T
            r0 = pl.multiple_of(d * half + g * SSD_HEADS_PER_GROUP, SSD_HEADS_PER_GROUP)
            p_rows = row_ref[0, pl.ds(r0, SSD_HEADS_PER_GROUP), :]
            dt_rows = row_ref[1, pl.ds(r0, SSD_HEADS_PER_GROUP), :]
            w_rows = row_ref[2, pl.ds(r0, SSD_HEADS_PER_GROUP), :]
            dec_rows = row_ref[3, pl.ds(r0, SSD_HEADS_PER_GROUP), :]
            ep_hi = ep_cols.astype(BF16)
            ep_lo = (ep_cols - ep_hi.astype(F32)).astype(BF16)
            inter = (_dot(ep_hi, expand[d]) + _dot(ep_lo, expand[d])) * _dot(cg, st_ref[d, g].astype(BF16))

            def head_terms(hh):
                li = d * half + hh
                seg = p_cols[:, li:li + 1] - p_rows[hh:hh + 1, :]
                l_mat = jnp.exp2(jnp.minimum(seg, 0.0)) * (cb_masked * dt_rows[hh:hh + 1, :])
                btw = bt * w_rows[hh:hh + 1, :]
                return l_mat.astype(BF16), btw.astype(BF16)

            for hp in range(SSD_HEADS_PER_GROUP // 2):
                h0, h1 = 2 * hp, 2 * hp + 1
                lo = (gg * SSD_HEADS_PER_GROUP + h0) * hd
                xp = x_ref[:, lo:lo + 2 * hd]
                x0 = jnp.where(low_lanes, xp, zero_b)
                x1 = jnp.where(low_lanes, zero_b, xp)
                sp = st_ref[d, g, :, h0 * hd:(h0 + 2) * hd]
                l0, w0 = head_terms(h0)
                l1, w1 = head_terms(h1)
                x_rhs = jnp.concatenate([x0, x1], axis=0)
                y_pair = _dot(jnp.concatenate([l0, l1], axis=1), x_rhs) + inter[:, h0 * hd:(h0 + 2) * hd]
                if d == 0:
                    y_pair = y_pair + dskip_ref[0:1, lo:lo + 2 * hd] * xp.astype(F32)
                y_ref[:, lo:lo + 2 * hd] = y_pair.astype(y_ref.dtype)
                dec = jnp.where(low_lanes[0:1], dec_rows[h0:h0 + 1, :], dec_rows[h1:h1 + 1, :])
                st_ref[d, g, :, h0 * hd:(h0 + 2) * hd] = sp * dec + _dot(jnp.concatenate([w0, w1], axis=1), x_rhs)


def _ssd_scan(act, small, ssd_prm, d_skip, batch, seq):
    t = act.shape[0]
    q = SSD_CHUNK
    nc = seq // q
    gps = SSD_N_GROUPS
    xw = gps * SSD_GROUP_WIDTH
    bw = gps * SSD_D_STATE
    b0 = ACT_B // bw
    c0 = ACT_C // bw
    dtc = SMALL_DT // LANE

    def fwd(col0):
        return lambda b, k, gi: (b * nc + k, col0 + gi)

    def bwd(col0):
        return lambda b, k, gi: (b * nc + nc - 1 - k, col0 + gi)

    blocks = 2 * [((q, xw), BF16), ((q, bw), BF16), ((q, bw), BF16), ((q, LANE), F32), ((q, xw), BF16)] + [((8, LANE), F32), ((8, xw), F32)]
    scratch = [
        pltpu.VMEM((2, SSD_N_GROUPS, SSD_D_STATE, SSD_GROUP_WIDTH), F32),
        pltpu.VMEM((2, q, LANE), F32),
        pltpu.VMEM((4, LANE, q), F32),
    ]
    scratch_bytes = _nbytes((2, SSD_N_GROUPS, SSD_D_STATE, SSD_GROUP_WIDTH), F32) + 6 * _nbytes((q, LANE), F32)
    return pl.pallas_call(
        functools.partial(_ssd_kernel, groups_per_step=gps),
        grid=(batch, nc, SSD_N_GROUPS // gps),
        in_specs=[
            pl.BlockSpec((q, xw), fwd(0)), pl.BlockSpec((q, bw), fwd(b0)), pl.BlockSpec((q, bw), fwd(c0)),
            pl.BlockSpec((q, LANE), lambda b, k, gi: (b * nc + k, dtc)),
            pl.BlockSpec((q, xw), bwd(0)), pl.BlockSpec((q, bw), bwd(b0)), pl.BlockSpec((q, bw), bwd(c0)),
            pl.BlockSpec((q, LANE), lambda b, k, gi: (b * nc + nc - 1 - k, dtc)),
            pl.BlockSpec((8, LANE), lambda b, k, gi: (0, 0)),
            pl.BlockSpec((8, xw), lambda b, k, gi: (0, gi)),
        ],
        out_specs=[pl.BlockSpec((q, xw), fwd(0)), pl.BlockSpec((q, xw), bwd(0))],
        out_shape=[jax.ShapeDtypeStruct((t, SSD_D_INNER), BF16)] * 2,
        scratch_shapes=scratch,
        compiler_params=_params(("arbitrary",) * 3, blocks, scratch_bytes, temp_bytes=64 * _nbytes((q, q), F32)),
        name="ssd_scan",
    )(act, act, act, small, act, act, act, small, ssd_prm, d_skip)


def _ssd_finish_kernel(yf_ref, yb_ref, z_ref, prm_ref, o_ref):
    z = z_ref[...].astype(F32)
    y = yf_ref[...].astype(F32) + yb_ref[...].astype(F32)
    y = y * (z * jax.nn.sigmoid(z))
    gw = SSD_GROUP_WIDTH
    for g in range(SSD_N_GROUPS):
        yg = y[:, g * gw:(g + 1) * gw]
        ms = jnp.mean(yg * yg, axis=-1, keepdims=True)
        o_ref[:, g * gw:(g + 1) * gw] = (yg * lax.rsqrt(ms + RMS_EPS) * prm_ref[0:1, g * gw:(g + 1) * gw]).astype(o_ref.dtype)


def _ssd_finish(yf, yb, main, fin_prm):
    t = yf.shape[0]
    ts = _tile(t, 256)
    di = SSD_D_INNER
    row_spec = pl.BlockSpec((ts, di), lambda i: (i, 0))
    blocks = 4 * [((ts, di), BF16)] + [((8, di), F32)]
    return pl.pallas_call(
        _ssd_finish_kernel,
        grid=(t // ts,),
        in_specs=[row_spec, row_spec, row_spec, pl.BlockSpec((8, di), lambda i: (0, 0))],
        out_specs=row_spec,
        out_shape=jax.ShapeDtypeStruct((t, di), BF16),
        compiler_params=_params(("arbitrary",), blocks, temp_bytes=4 * _nbytes((ts, di), F32)),
        name="ssd_finish",
    )(yf, yb, main, fin_prm)


def _rope_lanes(t, tab):
    return (t * tab[:, 0:LANE] + pltpu.roll(t, LANE - MLA_ROPE_DIM // 2, 1) * tab[:, LANE:2 * LANE]
            + pltpu.roll(t, MLA_ROPE_DIM // 2, 1) * tab[:, 2 * LANE:3 * LANE])


def _rms_rows(x, w):
    return x * lax.rsqrt(jnp.mean(x * x, axis=-1, keepdims=True) + RMS_EPS) * w


def _qproj_kernel(x_ref, nw_ref, w_ref, tab_ref, qt_ref, *, scale):
    xt = x_ref[...].T
    tm = xt.shape[1]
    inv = lax.rsqrt(jnp.mean(xt * xt, axis=0, keepdims=True) + RMS_EPS)
    nw = jnp.concatenate([nw_ref[...]] * (tm // LANE), axis=1)
    xnt = (xt * inv * nw).astype(BF16)
    half = MLA_ROPE_DIM // 2
    cos = tab_ref[0:half, :]
    sin = tab_ref[half:2 * half, :]
    r0 = MLA_NOPE_DIM
    for h in range(MLA_N_HEADS):
        qt = _dot(w_ref[h * MLA_QK_PAD:(h + 1) * MLA_QK_PAD, :], xnt)
        x1 = qt[r0:r0 + half]
        x2 = qt[r0 + half:r0 + 2 * half]
        qt_ref[0, h, 0:r0, :] = (qt[0:r0] * scale).astype(qt_ref.dtype)
        qt_ref[0, h, r0:r0 + half, :] = ((x1 * cos - x2 * sin) * scale).astype(qt_ref.dtype)
        qt_ref[0, h, r0 + half:r0 + 2 * half, :] = ((x2 * cos + x1 * sin) * scale).astype(qt_ref.dtype)
        qt_ref[0, h, r0 + 2 * half:MLA_QK_PAD, :] = jnp.zeros((MLA_QK_PAD - MLA_QK_DIM, tm), qt_ref.dtype)


def _kvproj_kernel(x_ref, kr_ref, nw_ref, wk_ref, wvt_ref, tab_ref, k_ref, vt_ref, *, tk):
    cn = _rms_rows(x_ref[...], nw_ref[0:1, :])
    tm = cn.shape[0]
    k_all = _dot(cn.astype(BF16), wk_ref[...])
    vt_all = _dot(wvt_ref[...], cn.T.astype(BF16))
    k_rope = _rope_lanes(kr_ref[...], tab_ref[...]).astype(k_ref.dtype)
    pad_rows = FLASH_VT_ROWS - MLA_V_DIM
    ones_rows = (lax.broadcasted_iota(jnp.int32, (pad_rows, tk), 0) == 0).astype(vt_ref.dtype)
    for h in range(MLA_N_HEADS):
        k_ref[0, h, :, 0:LANE] = k_all[:, h * MLA_NOPE_DIM:(h + 1) * MLA_NOPE_DIM].astype(k_ref.dtype)
        k_ref[0, h, :, LANE:2 * LANE] = k_rope
        for c in range(tm // tk):
            vt_ref[0, h, c, 0:MLA_V_DIM, :] = vt_all[h * MLA_V_DIM:(h + 1) * MLA_V_DIM, c * tk:(c + 1) * tk].astype(vt_ref.dtype)
            vt_ref[0, h, c, MLA_V_DIM:FLASH_VT_ROWS, :] = ones_rows


def _mla_proj(small, q_nw_lanes, wq_t, kv_nw, wk, wv_t, rope_tab, rope_tab_t, batch, seq):
    t = small.shape[0]
    tk = _tile(seq, FLASH_TK)
    tm = _tile(seq, max(512, tk))
    nsb = seq // tm
    nh = MLA_N_HEADS
    qw = nh * MLA_QK_PAD
    kw = nh * MLA_NOPE_DIM
    vw = nh * MLA_V_DIM
    q_blocks = [((tm, MLA_Q_RANK), F32), ((MLA_Q_RANK, LANE), F32), ((qw, MLA_Q_RANK), BF16), ((MLA_ROPE_DIM, tm), F32),
                ((nh, MLA_QK_PAD, tm), BF16)]
    qt = pl.pallas_call(
        functools.partial(_qproj_kernel, scale=MLA_QK_DIM ** -0.5 * LOG2_E),
        grid=(t // tm,),
        in_specs=[pl.BlockSpec((tm, MLA_Q_RANK), lambda i: (i, SMALL_Q // MLA_Q_RANK)),
                  pl.BlockSpec((MLA_Q_RANK, LANE), lambda i: (0, 0)),
                  pl.BlockSpec((qw, MLA_Q_RANK), lambda i: (0, 0)),
                  pl.BlockSpec((MLA_ROPE_DIM, tm), lambda i: (0, i % nsb))],
        out_specs=pl.BlockSpec((1, nh, MLA_QK_PAD, tm), lambda i: (i // nsb, 0, 0, i % nsb)),
        out_shape=jax.ShapeDtypeStruct((batch, nh, MLA_QK_PAD, seq), BF16),
        compiler_params=_params(("arbitrary",), q_blocks, temp_bytes=6 * _nbytes((tm, MLA_Q_RANK), F32)),
        name="mla_q_proj",
    )(small, q_nw_lanes, wq_t, rope_tab_t)
    kv_blocks = [((tm, MLA_KV_RANK), F32), ((tm, LANE), F32), ((8, MLA_KV_RANK), F32), ((MLA_KV_RANK, kw), BF16),
                 ((vw, MLA_KV_RANK), BF16), ((tm, 3 * LANE), F32), ((nh, tm, MLA_QK_PAD), BF16), ((nh, tm // tk, FLASH_VT_ROWS, tk), BF16)]
    k, vt = pl.pallas_call(
        functools.partial(_kvproj_kernel, tk=tk),
        grid=(t // tm,),
        in_specs=[pl.BlockSpec((tm, MLA_KV_RANK), lambda i: (i, SMALL_CKV // MLA_KV_RANK)),
                  pl.BlockSpec((tm, LANE), lambda i: (i, SMALL_KROPE // LANE)),
                  pl.BlockSpec((8, MLA_KV_RANK), lambda i: (0, 0)),
                  pl.BlockSpec((MLA_KV_RANK, kw), lambda i: (0, 0)),
                  pl.BlockSpec((vw, MLA_KV_RANK), lambda i: (0, 0)),
                  pl.BlockSpec((tm, 3 * LANE), lambda i: (i % nsb, 0))],
        out_specs=[pl.BlockSpec((1, nh, tm, MLA_QK_PAD), lambda i: (i // nsb, 0, i % nsb, 0)),
                   pl.BlockSpec((1, nh, tm // tk, FLASH_VT_ROWS, tk), lambda i: (i // nsb, 0, i % nsb, 0, 0))],
        out_shape=[jax.ShapeDtypeStruct((batch, nh, seq, MLA_QK_PAD), BF16),
                   jax.ShapeDtypeStruct((batch, nh, seq // tk, FLASH_VT_ROWS, tk), BF16)],
        compiler_params=_params(("arbitrary",), kv_blocks, temp_bytes=2 * _nbytes((tm, kw), F32) + 4 * _nbytes((tm, MLA_KV_RANK), F32)),
        name="mla_kv_proj",
    )(small, small, kv_nw, wk, wv_t, rope_tab)
    return qt, k, vt


def _flash_kernel(qt_ref, k_ref, vt_ref, o_ref, s0_ref, s1_ref, p0_ref, p1_ref, acc_ref):
    tq = qt_ref.shape[3]
    n_kv, _, tk = vt_ref.shape[2:]

    def qk(j, s_ref):
        s = _dot(k_ref[0, 0, j * tk:(j + 1) * tk, :], qt_ref[0, 0])
        s_ref[...] = s
        return jnp.max(s, axis=0, keepdims=True)

    def soft(m, cm, s_ref, p_ref):
        m_new = jnp.maximum(m, cm)
        p_ref[...] = jnp.exp2(s_ref[...] - m_new).astype(p_ref.dtype)
        return m_new, jnp.exp2(m - m_new)

    def pv(j, alpha, p_ref):
        acc_ref[...] = alpha * acc_ref[...] + _dot(vt_ref[0, 0, j], p_ref[...])

    def step(j, m, alpha_prev, cm, s_cur, s_nxt, p_prev, p_cur):
        cm_next = qk(j + 1, s_nxt)
        pv(j - 1, alpha_prev, p_prev)
        m, alpha = soft(m, cm, s_cur, p_cur)
        return m, alpha, cm_next

    acc_ref[...] = jnp.zeros(acc_ref.shape, acc_ref.dtype)
    m = jnp.full((1, tq), -jnp.inf, F32)
    m, alpha = soft(m, qk(0, s0_ref), s0_ref, p0_ref)
    cm = qk(1, s1_ref)
    for j in range(1, n_kv - 1, 2):
        m, alpha, cm = step(j, m, alpha, cm, s1_ref, s0_ref, p0_ref, p1_ref)
        m, alpha, cm = step(j + 1, m, alpha, cm, s0_ref, s1_ref, p1_ref, p0_ref)
    pv(n_kv - 2, alpha, p0_ref)
    m, alpha = soft(m, cm, s1_ref, p1_ref)
    pv(n_kv - 1, alpha, p1_ref)
    o_ref[...] = (acc_ref[0:MLA_V_DIM, :] / acc_ref[MLA_V_DIM:MLA_V_DIM + 1, :]).T.astype(o_ref.dtype)


def _flash_attention(qt, k, vt):
    batch, nh, seq, _ = k.shape
    n_kv, vt_rows, tk = vt.shape[2:]
    assert n_kv >= 2 and n_kv % 2 == 0, n_kv
    tq = _tile(seq, FLASH_TQ)
    nq = seq // tq
    blocks = [((MLA_QK_PAD, tq), BF16), ((seq, MLA_QK_PAD), BF16), ((n_kv, vt_rows, tk), BF16), ((tq, MLA_V_DIM), BF16)]
    scratch = [pltpu.VMEM((tk, tq), F32), pltpu.VMEM((tk, tq), F32), pltpu.VMEM((tk, tq), BF16), pltpu.VMEM((tk, tq), BF16),
               pltpu.VMEM((vt_rows, tq), F32)]
    scratch_bytes = 3 * _nbytes((tk, tq), F32) + _nbytes((vt_rows, tq), F32)
    return pl.pallas_call(
        _flash_kernel,
        grid=(batch, nh, nq),
        in_specs=[
            pl.BlockSpec((1, 1, MLA_QK_PAD, tq), lambda b, h, i: (b, h, 0, i)),
            pl.BlockSpec((1, 1, seq, MLA_QK_PAD), lambda b, h, i: (b, h, 0, 0)),
            pl.BlockSpec((1, 1, n_kv, vt_rows, tk), lambda b, h, i: (b, h, 0, 0, 0)),
        ],
        out_specs=pl.BlockSpec((tq, MLA_V_DIM), lambda b, h, i: (b * nq + i, h)),
        out_shape=jax.ShapeDtypeStruct((batch * seq, nh * MLA_V_DIM), BF16),
        scratch_shapes=scratch,
        compiler_params=_params(("arbitrary",) * 3, blocks, scratch_bytes, temp_bytes=4 * _nbytes((tk, tq), F32)),
        name="mla_flash_attention",
    )(qt, k, vt)


def _merge_kernel(a1_ref, w1_ref, a2_ref, w2_ref, g1_ref, g2_ref, o_ref):
    y1 = _dot(a1_ref[...], w1_ref[...])
    y2 = _dot(a2_ref[...], w2_ref[...])
    g1 = jax.nn.sigmoid(g1_ref[...].astype(F32))
    g2 = jax.nn.sigmoid(g2_ref[...].astype(F32))
    o_ref[...] = (g1 * y1 + g2 * y2).astype(o_ref.dtype)


def _merge(y_ssd_n, w_ssd_out, attn, w_mla_out, main, layer):
    t = y_ssd_n.shape[0]
    tm = _tile(t, 1024)
    tn = 512
    k1, k2 = SSD_D_INNER, MLA_N_HEADS * MLA_V_DIM
    streamed = [((tm, k1), BF16), ((tm, k2), BF16), ((tm, tn), BF16), ((tm, tn), BF16), ((tm, tn), BF16)]
    resident = _nbytes((k1, tn), BF16) + _nbytes((k2, tn), BF16)
    once = pl.Buffered(1)
    return pl.pallas_call(
        _merge_kernel,
        grid=(D_MODEL // tn, t // tm),
        in_specs=[
            pl.BlockSpec((tm, k1), lambda j, i: (i, 0)), pl.BlockSpec((None, k1, tn), lambda j, i: (layer, 0, j), pipeline_mode=once),
            pl.BlockSpec((tm, k2), lambda j, i: (i, 0)), pl.BlockSpec((None, k2, tn), lambda j, i: (layer, 0, j), pipeline_mode=once),
            pl.BlockSpec((tm, tn), lambda j, i: (i, MAIN_GSSD // tn + j)),
            pl.BlockSpec((tm, tn), lambda j, i: (i, MAIN_GMLA // tn + j)),
        ],
        out_specs=pl.BlockSpec((tm, tn), lambda j, i: (i, j)),
        out_shape=jax.ShapeDtypeStruct((t, D_MODEL), BF16),
        compiler_params=_params(("arbitrary",) * 2, streamed, scratch_bytes=resident, temp_bytes=4 * _nbytes((tm, tn), F32)),
        name="branch_merge",
    )(y_ssd_n, w_ssd_out, attn, w_mla_out, main, main)


def _layer_norm_rows(r, prm_ref):
    mu = jnp.mean(r, axis=-1, keepdims=True)
    c = r - mu
    var = jnp.mean(c * c, axis=-1, keepdims=True)
    return c * lax.rsqrt(var + LN_EPS) * prm_ref[0:1, :] + prm_ref[1:2, :]


def _out_ln_kernel(a_ref, w_ref, h_ref, prm_ref, of_ref, ob_ref):
    y = _layer_norm_rows(DEEPNORM_ALPHA * h_ref[...] + _dot(a_ref[...], w_ref[...]), prm_ref)
    of_ref[...] = y
    ob_ref[...] = y.astype(ob_ref.dtype)


def _out_ln(merged, w_out, h, ln_prm, layer):
    t = h.shape[0]
    tm = _tile(t, 512)
    d = D_MODEL
    streamed = [((tm, d), BF16), ((tm, d), F32), ((8, d), F32), ((tm, d), F32), ((tm, d), BF16)]
    row = lambda i: (i, 0)
    return pl.pallas_call(
        _out_ln_kernel,
        grid=(t // tm,),
        in_specs=[pl.BlockSpec((tm, d), row), pl.BlockSpec((None, d, d), lambda i: (layer, 0, 0), pipeline_mode=pl.Buffered(1)), pl.BlockSpec((tm, d), row),
                  pl.BlockSpec((8, d), lambda i: (0, 0))],
        out_specs=[pl.BlockSpec((tm, d), row), pl.BlockSpec((tm, d), row)],
        out_shape=[jax.ShapeDtypeStruct((t, d), F32), jax.ShapeDtypeStruct((t, d), BF16)],
        compiler_params=_params(("arbitrary",), streamed, scratch_bytes=_nbytes((d, d), BF16), temp_bytes=3 * _nbytes((tm, d), F32)),
        name="mixer_out_ln",
    )(merged, w_out, h, ln_prm)


def _ffn_up_kernel(a_ref, wg_ref, wu_ref, o_ref):
    a = a_ref[...]
    gate = _dot(a, wg_ref[...])
    o_ref[...] = (gate * jax.nn.sigmoid(gate) * _dot(a, wu_ref[...])).astype(o_ref.dtype)


def _ffn_up(hb, wg, wu, layer):
    t = hb.shape[0]
    tm = _tile(t, 1024)
    tn = 512
    d = D_MODEL
    blocks = [((tm, d), BF16), ((d, tn), BF16), ((d, tn), BF16), ((tm, tn), BF16)]
    return pl.pallas_call(
        _ffn_up_kernel,
        grid=(FFN_HIDDEN // tn, t // tm),
        in_specs=[pl.BlockSpec((tm, d), lambda j, i: (i, 0)), pl.BlockSpec((None, d, tn), lambda j, i: (layer, 0, j)),
                  pl.BlockSpec((None, d, tn), lambda j, i: (layer, 0, j))],
        out_specs=pl.BlockSpec((tm, tn), lambda j, i: (i, j)),
        out_shape=jax.ShapeDtypeStruct((t, FFN_HIDDEN), BF16),
        compiler_params=_params(("arbitrary",) * 2, blocks, temp_bytes=3 * _nbytes((tm, tn), F32)),
        name="ffn_up",
    )(hb, wg, wu)


def _ffn_down_ln_kernel(a_ref, w_ref, h_ref, prm_ref, of_ref, ob_ref):
    y = _layer_norm_rows(DEEPNORM_ALPHA * h_ref[...] + _dot(a_ref[...], w_ref[...]), prm_ref)
    of_ref[...] = y
    ob_ref[...] = y.astype(ob_ref.dtype)


def _ffn_down_ln(act, w_down, h, ln_prm, layer):
    t = h.shape[0]
    tm = _tile(t, 256)
    d = D_MODEL
    kk = FFN_HIDDEN
    row = lambda i: (i, 0)
    streamed = [((tm, kk), BF16), ((tm, d), F32), ((8, d), F32), ((tm, d), F32), ((tm, d), BF16)]
    return pl.pallas_call(
        _ffn_down_ln_kernel,
        grid=(t // tm,),
        in_specs=[pl.BlockSpec((tm, kk), row), pl.BlockSpec((None, kk, d), lambda i: (layer, 0, 0), pipeline_mode=pl.Buffered(1)),
                  pl.BlockSpec((tm, d), row), pl.BlockSpec((8, d), lambda i: (0, 0))],
        out_specs=[pl.BlockSpec((tm, d), row), pl.BlockSpec((tm, d), row)],
        out_shape=[jax.ShapeDtypeStruct((t, d), F32), jax.ShapeDtypeStruct((t, d), BF16)],
        compiler_params=_params(("arbitrary",), streamed, scratch_bytes=_nbytes((kk, d), BF16), temp_bytes=3 * _nbytes((tm, d), F32)),
        name="ffn_down_ln",
    )(act, w_down, h, ln_prm)


def _rows8(*rows):
    n = rows[0].shape[0]
    pad = jnp.zeros((8 - len(rows), n), F32)
    return jnp.concatenate([jnp.stack([r.astype(F32) for r in rows]), pad], axis=0)


def _rope_table(seq):
    pos = jnp.arange(seq, dtype=F32)
    inv_freq = ROPE_THETA ** (-jnp.arange(0, MLA_ROPE_DIM, 2, dtype=F32) / MLA_ROPE_DIM)
    ang = pos[:, None] * inv_freq[None, :]
    cos, sin = jnp.cos(ang), jnp.sin(ang)
    half = MLA_ROPE_DIM // 2
    z = lambda n: jnp.zeros((seq, n), F32)
    lanes = jnp.concatenate([cos, cos, z(LANE - 2 * half), -sin, z(LANE - half), z(half), sin, z(LANE - 2 * half)], axis=1)
    return lanes, jnp.concatenate([cos.T, sin.T], axis=0)


def _q_weight_t(w_uq):
    r = w_uq.shape[0]
    w = w_uq.reshape(r, MLA_N_HEADS, MLA_QK_DIM)
    w = jnp.pad(w, ((0, 0), (0, 0), (0, MLA_QK_PAD - MLA_QK_DIM)))
    return w.reshape(r, MLA_N_HEADS * MLA_QK_PAD).T.astype(BF16)


def _kv_weights(w_ukv):
    r = w_ukv.shape[0]
    w = w_ukv.reshape(r, MLA_N_HEADS, MLA_NOPE_DIM + MLA_V_DIM)
    wk = w[:, :, :MLA_NOPE_DIM].reshape(r, MLA_N_HEADS * MLA_NOPE_DIM)
    wv = w[:, :, MLA_NOPE_DIM:].reshape(r, MLA_N_HEADS * MLA_V_DIM)
    return wk.astype(BF16), wv.T.astype(BF16)


def kernel(x, w_in, conv_w, conv_b, ssd_a_log, ssd_dt_bias, ssd_d, ssd_norm_w, w_ssd_out, mla_q_norm_w, w_uq,
           mla_kv_norm_w, w_ukv, w_mla_out, w_out, ln1_g, ln1_b, w_ffn_gate, w_ffn_up, w_ffn_down, ln2_g, ln2_b):
    batch, seq, d = x.shape
    assert d == D_MODEL and seq % SSD_CHUNK == 0
    t = batch * seq
    hf = x.reshape(t, d)
    hb = hf
    rope_tab, rope_tab_t = _rope_table(seq)
    dt0 = MAIN_GSSD + 2 * SSD_N_HEADS
    q0 = dt0
    kv0 = q0 + MLA_Q_RANK
    kr0 = kv0 + MLA_KV_RANK
    g0 = kr0 + MLA_ROPE_DIM
    w_ssd_out_b, w_mla_out_b, w_out_b = w_ssd_out.astype(BF16), w_mla_out.astype(BF16), w_out.astype(BF16)
    w_gate_b, w_up_b, w_down_b = w_ffn_gate.astype(BF16), w_ffn_up.astype(BF16), w_ffn_down.astype(BF16)
    for l in range(w_in.shape[0]):
        wi = w_in[l]
        w_main = jnp.concatenate([wi[:, :MAIN_GSSD], wi[:, g0:]], axis=1).astype(BF16)
        w_small = jnp.concatenate(
            [wi[:, q0:kv0], wi[:, kv0:kr0], wi[:, MAIN_GSSD:dt0], wi[:, kr0:g0], jnp.zeros((d, LANE - MLA_ROPE_DIM), wi.dtype)],
            axis=1).astype(BF16)
        main = _matmul(hb, w_main, BF16, 1024, "in_proj_main")
        small = _matmul(hb, w_small, F32, SMALL_WIDTH, "in_proj_small")

        act = _conv_silu(main, _rows8(*[conv_w[l, i] for i in range(SSD_CONV_WIDTH)], conv_b[l]), batch, seq)
        yf, yb = _ssd_scan(act, small, _rows8(ssd_dt_bias[l].reshape(-1), ssd_a_log[l].reshape(-1)),
                           _rows8(jnp.repeat(ssd_d[l], SSD_HEAD_DIM)), batch, seq)
        y_ssd_n = _ssd_finish(yf, yb, main, _rows8(ssd_norm_w[l]))

        wk, wv_t = _kv_weights(w_ukv[l])
        q_nw_lanes = jnp.broadcast_to(mla_q_norm_w[l].astype(F32)[:, None], (MLA_Q_RANK, LANE))
        qt, k, vt = _mla_proj(small, q_nw_lanes, _q_weight_t(w_uq[l]), _rows8(mla_kv_norm_w[l]), wk, wv_t,
                              rope_tab, rope_tab_t, batch, seq)
        attn = _flash_attention(qt, k, vt)

        merged = _merge(y_ssd_n, w_ssd_out_b, attn, w_mla_out_b, main, l)
        hf, hb = _out_ln(merged, w_out_b, hf, _rows8(ln1_g[l], ln1_b[l]), l)
        ffn_act = _ffn_up(hb, w_gate_b, w_up_b, l)
        hf, hb = _ffn_down_ln(ffn_act, w_down_b, hf, _rows8(ln2_g[l], ln2_b[l]), l)
    return hf.reshape(batch, seq, d)
```

```python
import functools

import jax
import jax.numpy as jnp
from jax import lax
from jax.experimental import pallas as pl
from jax.experimental.pallas import tpu as pltpu

F32 = jnp.float32
BF16 = jnp.bfloat16

D_MODEL = 2048
DEPTH = 4
SSD_D_INNER = 4096
SSD_HEAD_DIM = 64
SSD_N_HEADS = 64
SSD_N_GROUPS = 8
SSD_HEADS_PER_GROUP = 8
SSD_D_STATE = 128
SSD_CONV_WIDTH = 5
SSD_CHUNK = 128
SSD_GROUP_WIDTH = SSD_HEADS_PER_GROUP * SSD_HEAD_DIM
SSD_BC_WIDTH = SSD_N_GROUPS * SSD_D_STATE
SSD_CONV_DIM = SSD_D_INNER + 2 * SSD_BC_WIDTH
MLA_N_HEADS = 16
MLA_Q_RANK = 512
MLA_KV_RANK = 512
MLA_NOPE_DIM = 128
MLA_ROPE_DIM = 64
MLA_V_DIM = 128
MLA_QK_DIM = MLA_NOPE_DIM + MLA_ROPE_DIM
MLA_QK_PAD = 256
ROPE_THETA = 10000.0
FLASH_TQ = 512
FLASH_TK = 256
FLASH_VT_ROWS = MLA_V_DIM + 16
LOG2_E = 1.4426950408889634
FFN_HIDDEN = 5632
DEEPNORM_ALPHA = (2 * DEPTH) ** 0.25
RMS_EPS = 1e-6
LN_EPS = 1e-5

MAIN_Z = 0
MAIN_XBC = SSD_D_INNER
MAIN_GSSD = MAIN_XBC + SSD_CONV_DIM
MAIN_GMLA = MAIN_GSSD + D_MODEL
MAIN_WIDTH = MAIN_GMLA + D_MODEL
SMALL_Q = 0
SMALL_CKV = MLA_Q_RANK
SMALL_DT = SMALL_CKV + MLA_KV_RANK
SMALL_KROPE = SMALL_DT + 2 * SSD_N_HEADS
SMALL_WIDTH = SMALL_KROPE + 128
ACT_B = SSD_D_INNER
ACT_C = SSD_D_INNER + SSD_BC_WIDTH

V7X_VMEM_BYTES = 64 * 1024 * 1024
VMEM_LIMIT_CAP = 56 * 1024 * 1024
VMEM_LIMIT_FLOOR = 32 * 1024 * 1024
LANE = 128
BF16_SUBLANE_TILE = 16


def _nbytes(shape, dtype):
    n = 1
    for s in shape:
        n *= s
    return n * jnp.dtype(dtype).itemsize


def _params(semantics, blocks, scratch_bytes=0, temp_bytes=0):
    need = 2 * sum(_nbytes(s, d) for s, d in blocks) + scratch_bytes + temp_bytes
    limit = min(max(need + need // 4, VMEM_LIMIT_FLOOR), VMEM_LIMIT_CAP)
    return pltpu.CompilerParams(dimension_semantics=semantics, vmem_limit_bytes=limit)


def _tile(n, pref):
    t = min(n, pref)
    assert n % t == 0, (n, pref)
    return t


def _dot(a, b):
    return jnp.dot(a, b, preferred_element_type=F32)


def _dot_nt(a, b):
    return lax.dot_general(a, b, (((1,), (1,)), ((), ())), preferred_element_type=F32)


def _mm_kernel(a_ref, w_ref, o_ref):
    o_ref[...] = _dot(a_ref[...].astype(BF16), w_ref[...]).astype(o_ref.dtype)


def _matmul(a, w, out_dtype, tn, name):
    m, k = a.shape
    n = w.shape[1]
    tm = _tile(m, 2048 if a.dtype == BF16 else 1024)
    tn = _tile(n, tn)
    blocks = [((tm, k), a.dtype), ((k, tn), w.dtype), ((tm, tn), out_dtype)]
    return pl.pallas_call(
        _mm_kernel,
        grid=(n // tn, m // tm),
        in_specs=[pl.BlockSpec((tm, k), lambda j, i: (i, 0)), pl.BlockSpec((k, tn), lambda j, i: (0, j))],
        out_specs=pl.BlockSpec((tm, tn), lambda j, i: (i, j)),
        out_shape=jax.ShapeDtypeStruct((m, n), out_dtype),
        compiler_params=_params(("arbitrary", "arbitrary"), blocks, temp_bytes=_nbytes((tm, tn), F32)),
        name=name,
    )(a, w)


def _conv_kernel(prev_ref, cur_ref, next_ref, w_ref, o_ref, *, n_seq_blocks):
    j = pl.program_id(1)
    cur = cur_ref[...].astype(F32)
    ts = cur.shape[0]
    has_prev = j > 0
    has_next = j < n_seq_blocks - 1
    halo = BF16_SUBLANE_TILE
    p0 = jnp.where(has_prev, prev_ref[halo - 2:halo - 1, :].astype(F32), 0.0)
    p1 = jnp.where(has_prev, prev_ref[halo - 1:halo, :].astype(F32), 0.0)
    n0 = jnp.where(has_next, next_ref[0:1, :].astype(F32), 0.0)
    n1 = jnp.where(has_next, next_ref[1:2, :].astype(F32), 0.0)
    row = lax.broadcasted_iota(jnp.int32, cur.shape, 0)
    xm1 = jnp.where(row == 0, p1, pltpu.roll(cur, 1, 0))
    xm2 = jnp.where(row == 0, p0, jnp.where(row == 1, p1, pltpu.roll(cur, 2, 0)))
    xp1 = jnp.where(row == ts - 1, n0, pltpu.roll(cur, ts - 1, 0))
    xp2 = jnp.where(row == ts - 2, n0, jnp.where(row == ts - 1, n1, pltpu.roll(cur, ts - 2, 0)))
    w = w_ref[...]
    y = xm2 * w[0:1] + xm1 * w[1:2] + cur * w[2:3] + xp1 * w[3:4] + xp2 * w[4:5] + w[5:6]
    o_ref[...] = (y * jax.nn.sigmoid(y)).astype(o_ref.dtype)


def _conv_silu(main, conv_prm, batch, seq):
    t = main.shape[0]
    ts = _tile(seq, 512)
    tc = 1024
    nsb = seq // ts
    col0 = MAIN_XBC // tc
    halo = BF16_SUBLANE_TILE
    rpb = ts // halo
    last_halo = t // halo - 1
    blocks = [((halo, tc), BF16), ((ts, tc), BF16), ((halo, tc), BF16), ((8, tc), F32), ((ts, tc), BF16)]
    return pl.pallas_call(
        functools.partial(_conv_kernel, n_seq_blocks=nsb),
        grid=(batch, nsb, SSD_CONV_DIM // tc),
        in_specs=[
            pl.BlockSpec((halo, tc), lambda b, j, c: (jnp.maximum((b * nsb + j) * rpb - 1, 0), col0 + c)),
            pl.BlockSpec((ts, tc), lambda b, j, c: (b * nsb + j, col0 + c)),
            pl.BlockSpec((halo, tc), lambda b, j, c: (jnp.minimum((b * nsb + j + 1) * rpb, last_halo), col0 + c)),
            pl.BlockSpec((8, tc), lambda b, j, c: (0, c)),
        ],
        out_specs=pl.BlockSpec((ts, tc), lambda b, j, c: (b * nsb + j, c)),
        out_shape=jax.ShapeDtypeStruct((t, SSD_CONV_DIM), BF16),
        compiler_params=_params(("arbitrary",) * 3, blocks, temp_bytes=8 * _nbytes((ts, tc), F32)),
        name="conv_silu",
    )(main, main, main, conv_prm)


def _ssd_kernel(xf_ref, bf_ref, cf_ref, dtf_ref, xb_ref, bb_ref, cb_ref, dtb_ref, prm_ref, dskip_ref,
                yf_ref, yb_ref, st_ref, col_ref, row_ref, *, groups_per_step):
    q = SSD_CHUNK
    k = pl.program_id(1)
    gi = pl.program_id(2)
    half = SSD_N_HEADS

    @pl.when(jnp.logical_and(k == 0, gi == 0))
    def _zero_state():
        st_ref[...] = jnp.zeros(st_ref.shape, st_ref.dtype)

    @pl.when(gi == 0)
    def _per_chunk_decay_terms():
        lane = lax.broadcasted_iota(jnp.int32, (q, 2 * half), 1)
        is_fwd = lane < half
        x = jnp.where(is_fwd, dtf_ref[...], dtb_ref[...]) + prm_ref[0:1, :]
        dt = jnp.maximum(x, 0.0) + jnp.log1p(jnp.exp(-jnp.abs(x)))
        a = -jnp.exp(prm_ref[1:2, :]) * dt
        r = lax.broadcasted_iota(jnp.int32, (q, q), 0)
        c = lax.broadcasted_iota(jnp.int32, (q, q), 1)
        tril = (r >= c).astype(BF16)
        triu = (r <= c).astype(BF16)
        a_hi = a.astype(BF16)
        a_lo = (a - a_hi.astype(F32)).astype(BF16)
        p = jnp.where(is_fwd, _dot(tril, a_hi) + _dot(tril, a_lo), _dot(triu, a_hi) + _dot(triu, a_lo))
        p_tot = jnp.where(is_fwd[0:1], p[q - 1:q, :], p[0:1, :])
        p_seg = p * LOG2_E
        col_ref[0] = p_seg
        col_ref[1] = jnp.exp(p)
        row_ref[0] = p_seg.T
        row_ref[1] = dt.T
        row_ref[2] = (dt * jnp.exp(p_tot - p)).T
        row_ref[3] = jnp.broadcast_to(jnp.exp(p_tot), (q, 2 * half)).T

    ri = lax.broadcasted_iota(jnp.int32, (q, q), 0)
    ci = lax.broadcasted_iota(jnp.int32, (q, q), 1)
    masks = (ri >= ci, ri <= ci)
    low_lanes = ci < SSD_HEAD_DIM
    dirs = ((xf_ref, bf_ref, cf_ref, yf_ref), (xb_ref, bb_ref, cb_ref, yb_ref))
    hd = SSD_HEAD_DIM
    zero_b = jnp.zeros((q, 2 * hd), BF16)
    er = lax.broadcasted_iota(jnp.int32, (2 * half, SSD_GROUP_WIDTH), 0)
    ec = lax.broadcasted_iota(jnp.int32, (2 * half, SSD_GROUP_WIDTH), 1)
    expand = tuple((er == d * half + ec // hd).astype(BF16) for d in range(2))
    for gg in range(groups_per_step):
        g = gi * groups_per_step + gg
        shift = lax.rem(2 * half - g * SSD_HEADS_PER_GROUP, 2 * half)
        p_cols = pltpu.roll(col_ref[0], shift, 1)
        ep_cols = pltpu.roll(col_ref[1], shift, 1)
        for d, (x_ref, b_ref, c_ref, y_ref) in enumerate(dirs):
            bg = b_ref[:, gg * SSD_D_STATE:(gg + 1) * SSD_D_STATE]
            cg = c_ref[:, gg * SSD_D_STATE:(gg + 1) * SSD_D_STATE]
            cb_masked = jnp.where(masks[d], _dot_nt(cg, bg), 0.0)
            bt = bg.astype(F32).T
            r0 = pl.multiple_of(d * half + g * SSD_HEADS_PER_GROUP, SSD_HEADS_PER_GROUP)
            p_rows = row_ref[0, pl.ds(r0, SSD_HEADS_PER_GROUP), :]
            dt_rows = row_ref[1, pl.ds(r0, SSD_HEADS_PER_GROUP), :]
            w_rows = row_ref[2, pl.ds(r0, SSD_HEADS_PER_GROUP), :]
            dec_rows = row_ref[3, pl.ds(r0, SSD_HEADS_PER_GROUP), :]
            inter = _dot(ep_cols.astype(BF16), expand[d]) * _dot(cg, st_ref[d, g].astype(BF16))

            def head_terms(hh):
                li = d * half + hh
                seg = p_cols[:, li:li + 1] - p_rows[hh:hh + 1, :]
                l_mat = jnp.exp2(jnp.minimum(seg, 0.0)) * (cb_masked * dt_rows[hh:hh + 1, :])
                btw = bt * w_rows[hh:hh + 1, :]
                return l_mat.astype(BF16), btw.astype(BF16)

            for hp in range(SSD_HEADS_PER_GROUP // 2):
                h0, h1 = 2 * hp, 2 * hp + 1
                lo = (gg * SSD_HEADS_PER_GROUP + h0) * hd
                xp = x_ref[:, lo:lo + 2 * hd]
                x0 = jnp.where(low_lanes, xp, zero_b)
                x1 = jnp.where(low_lanes, zero_b, xp)
                sp = st_ref[d, g, :, h0 * hd:(h0 + 2) * hd]
                l0, w0 = head_terms(h0)
                l1, w1 = head_terms(h1)
                x_rhs = jnp.concatenate([x0, x1], axis=0)
                y_pair = _dot(jnp.concatenate([l0, l1], axis=1), x_rhs) + inter[:, h0 * hd:(h0 + 2) * hd]
                if d == 0:
                    y_pair = y_pair + dskip_ref[0:1, lo:lo + 2 * hd] * xp.astype(F32)
                y_ref[:, lo:lo + 2 * hd] = y_pair.astype(y_ref.dtype)
                dec = jnp.where(low_lanes[0:1], dec_rows[h0:h0 + 1, :], dec_rows[h1:h1 + 1, :])
                st_ref[d, g, :, h0 * hd:(h0 + 2) * hd] = sp * dec + _dot(jnp.concatenate([w0, w1], axis=1), x_rhs)


def _ssd_scan(act, small, ssd_prm, d_skip, batch, seq):
    t = act.shape[0]
    q = SSD_CHUNK
    nc = seq // q
    gps = SSD_N_GROUPS
    xw = gps * SSD_GROUP_WIDTH
    bw = gps * SSD_D_STATE
    b0 = ACT_B // bw
    c0 = ACT_C // bw
    dtc = SMALL_DT // LANE

    def fwd(col0):
        return lambda b, k, gi: (b * nc + k, col0 + gi)

    def bwd(col0):
        return lambda b, k, gi: (b * nc + nc - 1 - k, col0 + gi)

    blocks = 2 * [((q, xw), BF16), ((q, bw), BF16), ((q, bw), BF16), ((q, LANE), F32), ((q, xw), BF16)] + [((8, LANE), F32), ((8, xw), F32)]
    scratch = [
        pltpu.VMEM((2, SSD_N_GROUPS, SSD_D_STATE, SSD_GROUP_WIDTH), F32),
        pltpu.VMEM((2, q, LANE), F32),
        pltpu.VMEM((4, LANE, q), F32),
    ]
    scratch_bytes = _nbytes((2, SSD_N_GROUPS, SSD_D_STATE, SSD_GROUP_WIDTH), F32) + 6 * _nbytes((q, LANE), F32)
    return pl.pallas_call(
        functools.partial(_ssd_kernel, groups_per_step=gps),
        grid=(batch, nc, SSD_N_GROUPS // gps),
        in_specs=[
            pl.BlockSpec((q, xw), fwd(0)), pl.BlockSpec((q, bw), fwd(b0)), pl.BlockSpec((q, bw), fwd(c0)),
            pl.BlockSpec((q, LANE), lambda b, k, gi: (b * nc + k, dtc)),
            pl.BlockSpec((q, xw), bwd(0)), pl.BlockSpec((q, bw), bwd(b0)), pl.BlockSpec((q, bw), bwd(c0)),
            pl.BlockSpec((q, LANE), lambda b, k, gi: (b * nc + nc - 1 - k, dtc)),
            pl.BlockSpec((8, LANE), lambda b, k, gi: (0, 0)),
            pl.BlockSpec((8, xw), lambda b, k, gi: (0, gi)),
        ],
        out_specs=[pl.BlockSpec((q, xw), fwd(0)), pl.BlockSpec((q, xw), bwd(0))],
        out_shape=[jax.ShapeDtypeStruct((t, SSD_D_INNER), BF16)] * 2,
        scratch_shapes=scratch,
        compiler_params=_params(("arbitrary",) * 3, blocks, scratch_bytes, temp_bytes=64 * _nbytes((q, q), F32)),
        name="ssd_scan",
    )(act, act, act, small, act, act, act, small, ssd_prm, d_skip)


def _ssd_finish_kernel(yf_ref, yb_ref, z_ref, prm_ref, o_ref):
    z = z_ref[...].astype(F32)
    y = yf_ref[...].astype(F32) + yb_ref[...].astype(F32)
    y = y * (z * jax.nn.sigmoid(z))
    gw = SSD_GROUP_WIDTH
    for g in range(SSD_N_GROUPS):
        yg = y[:, g * gw:(g + 1) * gw]
        ms = jnp.mean(yg * yg, axis=-1, keepdims=True)
        o_ref[:, g * gw:(g + 1) * gw] = (yg * lax.rsqrt(ms + RMS_EPS) * prm_ref[0:1, g * gw:(g + 1) * gw]).astype(o_ref.dtype)


def _ssd_finish(yf, yb, main, fin_prm):
    t = yf.shape[0]
    ts = _tile(t, 256)
    di = SSD_D_INNER
    row_spec = pl.BlockSpec((ts, di), lambda i: (i, 0))
    blocks = 4 * [((ts, di), BF16)] + [((8, di), F32)]
    return pl.pallas_call(
        _ssd_finish_kernel,
        grid=(t // ts,),
        in_specs=[row_spec, row_spec, row_spec, pl.BlockSpec((8, di), lambda i: (0, 0))],
        out_specs=row_spec,
        out_shape=jax.ShapeDtypeStruct((t, di), BF16),
        compiler_params=_params(("arbitrary",), blocks, temp_bytes=4 * _nbytes((ts, di), F32)),
        name="ssd_finish",
    )(yf, yb, main, fin_prm)


def _rope_lanes(t, tab):
    return (t * tab[:, 0:LANE] + pltpu.roll(t, LANE - MLA_ROPE_DIM // 2, 1) * tab[:, LANE:2 * LANE]
            + pltpu.roll(t, MLA_ROPE_DIM // 2, 1) * tab[:, 2 * LANE:3 * LANE])


def _rms_rows(x, w):
    return x * lax.rsqrt(jnp.mean(x * x, axis=-1, keepdims=True) + RMS_EPS) * w


def _qproj_kernel(x_ref, nw_ref, w_ref, tab_ref, qt_ref, *, scale):
    xt = x_ref[...].T
    tm = xt.shape[1]
    inv = lax.rsqrt(jnp.mean(xt * xt, axis=0, keepdims=True) + RMS_EPS)
    nw = jnp.concatenate([nw_ref[...]] * (tm // LANE), axis=1)
    xnt = (xt * inv * nw).astype(BF16)
    half = MLA_ROPE_DIM // 2
    cos = tab_ref[0:half, :]
    sin = tab_ref[half:2 * half, :]
    r0 = MLA_NOPE_DIM
    for h in range(MLA_N_HEADS):
        qt = _dot(w_ref[h * MLA_QK_PAD:(h + 1) * MLA_QK_PAD, :], xnt)
        x1 = qt[r0:r0 + half]
        x2 = qt[r0 + half:r0 + 2 * half]
        qt_ref[0, h, 0:r0, :] = (qt[0:r0] * scale).astype(qt_ref.dtype)
        qt_ref[0, h, r0:r0 + half, :] = ((x1 * cos - x2 * sin) * scale).astype(qt_ref.dtype)
        qt_ref[0, h, r0 + half:r0 + 2 * half, :] = ((x2 * cos + x1 * sin) * scale).astype(qt_ref.dtype)
        qt_ref[0, h, r0 + 2 * half:MLA_QK_PAD, :] = jnp.zeros((MLA_QK_PAD - MLA_QK_DIM, tm), qt_ref.dtype)


def _kvproj_kernel(x_ref, kr_ref, nw_ref, wk_ref, wvt_ref, tab_ref, k_ref, vt_ref, *, tk):
    cn = _rms_rows(x_ref[...], nw_ref[0:1, :])
    tm = cn.shape[0]
    k_all = _dot(cn.astype(BF16), wk_ref[...])
    vt_all = _dot(wvt_ref[...], cn.T.astype(BF16))
    k_rope = _rope_lanes(kr_ref[...], tab_ref[...]).astype(k_ref.dtype)
    pad_rows = FLASH_VT_ROWS - MLA_V_DIM
    ones_rows = (lax.broadcasted_iota(jnp.int32, (pad_rows, tk), 0) == 0).astype(vt_ref.dtype)
    for h in range(MLA_N_HEADS):
        k_ref[0, h, :, 0:LANE] = k_all[:, h * MLA_NOPE_DIM:(h + 1) * MLA_NOPE_DIM].astype(k_ref.dtype)
        k_ref[0, h, :, LANE:2 * LANE] = k_rope
        for c in range(tm // tk):
            vt_ref[0, h, c, 0:MLA_V_DIM, :] = vt_all[h * MLA_V_DIM:(h + 1) * MLA_V_DIM, c * tk:(c + 1) * tk].astype(vt_ref.dtype)
            vt_ref[0, h, c, MLA_V_DIM:FLASH_VT_ROWS, :] = ones_rows


def _mla_proj(small, q_nw_lanes, wq_t, kv_nw, wk, wv_t, rope_tab, rope_tab_t, batch, seq):
    t = small.shape[0]
    tk = _tile(seq, FLASH_TK)
    tm = _tile(seq, max(512, tk))
    nsb = seq // tm
    nh = MLA_N_HEADS
    qw = nh * MLA_QK_PAD
    kw = nh * MLA_NOPE_DIM
    vw = nh * MLA_V_DIM
    q_blocks = [((tm, MLA_Q_RANK), F32), ((MLA_Q_RANK, LANE), F32), ((qw, MLA_Q_RANK), BF16), ((MLA_ROPE_DIM, tm), F32),
                ((nh, MLA_QK_PAD, tm), BF16)]
    qt = pl.pallas_call(
        functools.partial(_qproj_kernel, scale=MLA_QK_DIM ** -0.5 * LOG2_E),
        grid=(t // tm,),
        in_specs=[pl.BlockSpec((tm, MLA_Q_RANK), lambda i: (i, SMALL_Q // MLA_Q_RANK)),
                  pl.BlockSpec((MLA_Q_RANK, LANE), lambda i: (0, 0)),
                  pl.BlockSpec((qw, MLA_Q_RANK), lambda i: (0, 0)),
                  pl.BlockSpec((MLA_ROPE_DIM, tm), lambda i: (0, i % nsb))],
        out_specs=pl.BlockSpec((1, nh, MLA_QK_PAD, tm), lambda i: (i // nsb, 0, 0, i % nsb)),
        out_shape=jax.ShapeDtypeStruct((batch, nh, MLA_QK_PAD, seq), BF16),
        compiler_params=_params(("arbitrary",), q_blocks, temp_bytes=6 * _nbytes((tm, MLA_Q_RANK), F32)),
        name="mla_q_proj",
    )(small, q_nw_lanes, wq_t, rope_tab_t)
    kv_blocks = [((tm, MLA_KV_RANK), F32), ((tm, LANE), F32), ((8, MLA_KV_RANK), F32), ((MLA_KV_RANK, kw), BF16),
                 ((vw, MLA_KV_RANK), BF16), ((tm, 3 * LANE), F32), ((nh, tm, MLA_QK_PAD), BF16), ((nh, tm // tk, FLASH_VT_ROWS, tk), BF16)]
    k, vt = pl.pallas_call(
        functools.partial(_kvproj_kernel, tk=tk),
        grid=(t // tm,),
        in_specs=[pl.BlockSpec((tm, MLA_KV_RANK), lambda i: (i, SMALL_CKV // MLA_KV_RANK)),
                  pl.BlockSpec((tm, LANE), lambda i: (i, SMALL_KROPE // LANE)),
                  pl.BlockSpec((8, MLA_KV_RANK), lambda i: (0, 0)),
                  pl.BlockSpec((MLA_KV_RANK, kw), lambda i: (0, 0)),
                  pl.BlockSpec((vw, MLA_KV_RANK), lambda i: (0, 0)),
                  pl.BlockSpec((tm, 3 * LANE), lambda i: (i % nsb, 0))],
        out_specs=[pl.BlockSpec((1, nh, tm, MLA_QK_PAD), lambda i: (i // nsb, 0, i % nsb, 0)),
                   pl.BlockSpec((1, nh, tm // tk, FLASH_VT_ROWS, tk), lambda i: (i // nsb, 0, i % nsb, 0, 0))],
        out_shape=[jax.ShapeDtypeStruct((batch, nh, seq, MLA_QK_PAD), BF16),
                   jax.ShapeDtypeStruct((batch, nh, seq // tk, FLASH_VT_ROWS, tk), BF16)],
        compiler_params=_params(("arbitrary",), kv_blocks, temp_bytes=2 * _nbytes((tm, kw), F32) + 4 * _nbytes((tm, MLA_KV_RANK), F32)),
        name="mla_kv_proj",
    )(small, small, kv_nw, wk, wv_t, rope_tab)
    return qt, k, vt


def _flash_kernel(qt_ref, k_ref, vt_ref, o_ref, s0_ref, s1_ref, p0_ref, p1_ref, acc_ref):
    tq = qt_ref.shape[3]
    n_kv, _, tk = vt_ref.shape[2:]

    def qk(j, s_ref):
        s = _dot(k_ref[0, 0, j * tk:(j + 1) * tk, :], qt_ref[0, 0])
        s_ref[...] = s
        return jnp.max(s, axis=0, keepdims=True)

    def soft(m, cm, s_ref, p_ref):
        m_new = jnp.maximum(m, cm)
        p_ref[...] = jnp.exp2(s_ref[...] - m_new).astype(p_ref.dtype)
        return m_new, jnp.exp2(m - m_new)

    def pv(j, alpha, p_ref):
        acc_ref[...] = alpha * acc_ref[...] + _dot(vt_ref[0, 0, j], p_ref[...])

    def step(j, m, alpha_prev, cm, s_cur, s_nxt, p_prev, p_cur):
        cm_next = qk(j + 1, s_nxt)
        pv(j - 1, alpha_prev, p_prev)
        m, alpha = soft(m, cm, s_cur, p_cur)
        return m, alpha, cm_next

    acc_ref[...] = jnp.zeros(acc_ref.shape, acc_ref.dtype)
    m = jnp.full((1, tq), -jnp.inf, F32)
    m, alpha = soft(m, qk(0, s0_ref), s0_ref, p0_ref)
    cm = qk(1, s1_ref)
    for j in range(1, n_kv - 1, 2):
        m, alpha, cm = step(j, m, alpha, cm, s1_ref, s0_ref, p0_ref, p1_ref)
        m, alpha, cm = step(j + 1, m, alpha, cm, s0_ref, s1_ref, p1_ref, p0_ref)
    pv(n_kv - 2, alpha, p0_ref)
    m, alpha = soft(m, cm, s1_ref, p1_ref)
    pv(n_kv - 1, alpha, p1_ref)
    o_ref[...] = (acc_ref[0:MLA_V_DIM, :] / acc_ref[MLA_V_DIM:MLA_V_DIM + 1, :]).T.astype(o_ref.dtype)


def _flash_attention(qt, k, vt):
    batch, nh, seq, _ = k.shape
    n_kv, vt_rows, tk = vt.shape[2:]
    assert n_kv >= 2 and n_kv % 2 == 0, n_kv
    tq = _tile(seq, FLASH_TQ)
    nq = seq // tq
    blocks = [((MLA_QK_PAD, tq), BF16), ((seq, MLA_QK_PAD), BF16), ((n_kv, vt_rows, tk), BF16), ((tq, MLA_V_DIM), BF16)]
    scratch = [pltpu.VMEM((tk, tq), F32), pltpu.VMEM((tk, tq), F32), pltpu.VMEM((tk, tq), BF16), pltpu.VMEM((tk, tq), BF16),
               pltpu.VMEM((vt_rows, tq), F32)]
    scratch_bytes = 3 * _nbytes((tk, tq), F32) + _nbytes((vt_rows, tq), F32)
    return pl.pallas_call(
        _flash_kernel,
        grid=(batch, nh, nq),
        in_specs=[
            pl.BlockSpec((1, 1, MLA_QK_PAD, tq), lambda b, h, i: (b, h, 0, i)),
            pl.BlockSpec((1, 1, seq, MLA_QK_PAD), lambda b, h, i: (b, h, 0, 0)),
            pl.BlockSpec((1, 1, n_kv, vt_rows, tk), lambda b, h, i: (b, h, 0, 0, 0)),
        ],
        out_specs=pl.BlockSpec((tq, MLA_V_DIM), lambda b, h, i: (b * nq + i, h)),
        out_shape=jax.ShapeDtypeStruct((batch * seq, nh * MLA_V_DIM), BF16),
        scratch_shapes=scratch,
        compiler_params=_params(("arbitrary",) * 3, blocks, scratch_bytes, temp_bytes=4 * _nbytes((tk, tq), F32)),
        name="mla_flash_attention",
    )(qt, k, vt)


def _merge_kernel(a1_ref, w1_ref, a2_ref, w2_ref, g1_ref, g2_ref, o_ref):
    y1 = _dot(a1_ref[...], w1_ref[...])
    y2 = _dot(a2_ref[...], w2_ref[...])
    g1 = jax.nn.sigmoid(g1_ref[...].astype(F32))
    g2 = jax.nn.sigmoid(g2_ref[...].astype(F32))
    o_ref[...] = (g1 * y1 + g2 * y2).astype(o_ref.dtype)


def _merge(y_ssd_n, w_ssd_out, attn, w_mla_out, main, layer):
    t = y_ssd_n.shape[0]
    tm = _tile(t, 1024)
    tn = 512
    k1, k2 = SSD_D_INNER, MLA_N_HEADS * MLA_V_DIM
    streamed = [((tm, k1), BF16), ((tm, k2), BF16), ((tm, tn), BF16), ((tm, tn), BF16), ((tm, tn), BF16)]
    resident = _nbytes((k1, tn), BF16) + _nbytes((k2, tn), BF16)
    once = pl.Buffered(1)
    return pl.pallas_call(
        _merge_kernel,
        grid=(D_MODEL // tn, t // tm),
        in_specs=[
            pl.BlockSpec((tm, k1), lambda j, i: (i, 0)), pl.BlockSpec((None, k1, tn), lambda j, i: (layer, 0, j), pipeline_mode=once),
            pl.BlockSpec((tm, k2), lambda j, i: (i, 0)), pl.BlockSpec((None, k2, tn), lambda j, i: (layer, 0, j), pipeline_mode=once),
            pl.BlockSpec((tm, tn), lambda j, i: (i, MAIN_GSSD // tn + j)),
            pl.BlockSpec((tm, tn), lambda j, i: (i, MAIN_GMLA // tn + j)),
        ],
        out_specs=pl.BlockSpec((tm, tn), lambda j, i: (i, j)),
        out_shape=jax.ShapeDtypeStruct((t, D_MODEL), BF16),
        compiler_params=_params(("arbitrary",) * 2, streamed, scratch_bytes=resident, temp_bytes=4 * _nbytes((tm, tn), F32)),
        name="branch_merge",
    )(y_ssd_n, w_ssd_out, attn, w_mla_out, main, main)


def _layer_norm_rows(r, prm_ref):
    mu = jnp.mean(r, axis=-1, keepdims=True)
    c = r - mu
    var = jnp.mean(c * c, axis=-1, keepdims=True)
    return c * lax.rsqrt(var + LN_EPS) * prm_ref[0:1, :] + prm_ref[1:2, :]


def _out_ln_kernel(a_ref, w_ref, h_ref, prm_ref, of_ref, ob_ref):
    y = _layer_norm_rows(DEEPNORM_ALPHA * h_ref[...] + _dot(a_ref[...], w_ref[...]), prm_ref)
    of_ref[...] = y
    ob_ref[...] = y.astype(ob_ref.dtype)


def _out_ln(merged, w_out, h, ln_prm, layer):
    t = h.shape[0]
    tm = _tile(t, 512)
    d = D_MODEL
    streamed = [((tm, d), BF16), ((tm, d), F32), ((8, d), F32), ((tm, d), F32), ((tm, d), BF16)]
    row = lambda i: (i, 0)
    return pl.pallas_call(
        _out_ln_kernel,
        grid=(t // tm,),
        in_specs=[pl.BlockSpec((tm, d), row), pl.BlockSpec((None, d, d), lambda i: (layer, 0, 0), pipeline_mode=pl.Buffered(1)), pl.BlockSpec((tm, d), row),
                  pl.BlockSpec((8, d), lambda i: (0, 0))],
        out_specs=[pl.BlockSpec((tm, d), row), pl.BlockSpec((tm, d), row)],
        out_shape=[jax.ShapeDtypeStruct((t, d), F32), jax.ShapeDtypeStruct((t, d), BF16)],
        compiler_params=_params(("arbitrary",), streamed, scratch_bytes=_nbytes((d, d), BF16), temp_bytes=3 * _nbytes((tm, d), F32)),
        name="mixer_out_ln",
    )(merged, w_out, h, ln_prm)


def _ffn_up_kernel(a_ref, wg_ref, wu_ref, o_ref):
    a = a_ref[...]
    gate = _dot(a, wg_ref[...])
    o_ref[...] = (gate * jax.nn.sigmoid(gate) * _dot(a, wu_ref[...])).astype(o_ref.dtype)


def _ffn_up(hb, wg, wu, layer):
    t = hb.shape[0]
    tm = _tile(t, 1024)
    tn = 512
    d = D_MODEL
    blocks = [((tm, d), BF16), ((d, tn), BF16), ((d, tn), BF16), ((tm, tn), BF16)]
    return pl.pallas_call(
        _ffn_up_kernel,
        grid=(FFN_HIDDEN // tn, t // tm),
        in_specs=[pl.BlockSpec((tm, d), lambda j, i: (i, 0)), pl.BlockSpec((None, d, tn), lambda j, i: (layer, 0, j)),
                  pl.BlockSpec((None, d, tn), lambda j, i: (layer, 0, j))],
        out_specs=pl.BlockSpec((tm, tn), lambda j, i: (i, j)),
        out_shape=jax.ShapeDtypeStruct((t, FFN_HIDDEN), BF16),
        compiler_params=_params(("arbitrary",) * 2, blocks, temp_bytes=3 * _nbytes((tm, tn), F32)),
        name="ffn_up",
    )(hb, wg, wu)


def _ffn_down_ln_kernel(a_ref, w_ref, h_ref, prm_ref, of_ref, ob_ref):
    y = _layer_norm_rows(DEEPNORM_ALPHA * h_ref[...] + _dot(a_ref[...], w_ref[...]), prm_ref)
    of_ref[...] = y
    ob_ref[...] = y.astype(ob_ref.dtype)


def _ffn_down_ln(act, w_down, h, ln_prm, layer):
    t = h.shape[0]
    tm = _tile(t, 256)
    d = D_MODEL
    kk = FFN_HIDDEN
    row = lambda i: (i, 0)
    streamed = [((tm, kk), BF16), ((tm, d), F32), ((8, d), F32), ((tm, d), F32), ((tm, d), BF16)]
    return pl.pallas_call(
        _ffn_down_ln_kernel,
        grid=(t // tm,),
        in_specs=[pl.BlockSpec((tm, kk), row), pl.BlockSpec((None, kk, d), lambda i: (layer, 0, 0), pipeline_mode=pl.Buffered(1)),
                  pl.BlockSpec((tm, d), row), pl.BlockSpec((8, d), lambda i: (0, 0))],
        out_specs=[pl.BlockSpec((tm, d), row), pl.BlockSpec((tm, d), row)],
        out_shape=[jax.ShapeDtypeStruct((t, d), F32), jax.ShapeDtypeStruct((t, d), BF16)],
        compiler_params=_params(("arbitrary",), streamed, scratch_bytes=_nbytes((kk, d), BF16), temp_bytes=3 * _nbytes((tm, d), F32)),
        name="ffn_down_ln",
    )(act, w_down, h, ln_prm)


def _rows8(*rows):
    n = rows[0].shape[0]
    pad = jnp.zeros((8 - len(rows), n), F32)
    return jnp.concatenate([jnp.stack([r.astype(F32) for r in rows]), pad], axis=0)


def _rope_table(seq):
    pos = jnp.arange(seq, dtype=F32)
    inv_freq = ROPE_THETA ** (-jnp.arange(0, MLA_ROPE_DIM, 2, dtype=F32) / MLA_ROPE_DIM)
    ang = pos[:, None] * inv_freq[None, :]
    cos, sin = jnp.cos(ang), jnp.sin(ang)
    half = MLA_ROPE_DIM // 2
    z = lambda n: jnp.zeros((seq, n), F32)
    lanes = jnp.concatenate([cos, cos, z(LANE - 2 * half), -sin, z(LANE - half), z(half), sin, z(LANE - 2 * half)], axis=1)
    return lanes, jnp.concatenate([cos.T, sin.T], axis=0)


def _q_weight_t(w_uq):
    r = w_uq.shape[0]
    w = w_uq.reshape(r, MLA_N_HEADS, MLA_QK_DIM)
    w = jnp.pad(w, ((0, 0), (0, 0), (0, MLA_QK_PAD - MLA_QK_DIM)))
    return w.reshape(r, MLA_N_HEADS * MLA_QK_PAD).T.astype(BF16)


def _kv_weights(w_ukv):
    r = w_ukv.shape[0]
    w = w_ukv.reshape(r, MLA_N_HEADS, MLA_NOPE_DIM + MLA_V_DIM)
    wk = w[:, :, :MLA_NOPE_DIM].reshape(r, MLA_N_HEADS * MLA_NOPE_DIM)
    wv = w[:, :, MLA_NOPE_DIM:].reshape(r, MLA_N_HEADS * MLA_V_DIM)
    return wk.astype(BF16), wv.T.astype(BF16)


def kernel(x, w_in, conv_w, conv_b, ssd_a_log, ssd_dt_bias, ssd_d, ssd_norm_w, w_ssd_out, mla_q_norm_w, w_uq,
           mla_kv_norm_w, w_ukv, w_mla_out, w_out, ln1_g, ln1_b, w_ffn_gate, w_ffn_up, w_ffn_down, ln2_g, ln2_b):
    batch, seq, d = x.shape
    assert d == D_MODEL and seq % SSD_CHUNK == 0
    t = batch * seq
    hf = x.reshape(t, d)
    hb = hf
    rope_tab, rope_tab_t = _rope_table(seq)
    dt0 = MAIN_GSSD + 2 * SSD_N_HEADS
    q0 = dt0
    kv0 = q0 + MLA_Q_RANK
    kr0 = kv0 + MLA_KV_RANK
    g0 = kr0 + MLA_ROPE_DIM
    w_ssd_out_b, w_mla_out_b, w_out_b = w_ssd_out.astype(BF16), w_mla_out.astype(BF16), w_out.astype(BF16)
    w_gate_b, w_up_b, w_down_b = w_ffn_gate.astype(BF16), w_ffn_up.astype(BF16), w_ffn_down.astype(BF16)
    for l in range(w_in.shape[0]):
        wi = w_in[l]
        w_main = jnp.concatenate([wi[:, :MAIN_GSSD], wi[:, g0:]], axis=1).astype(BF16)
        w_small = jnp.concatenate(
            [wi[:, q0:kv0], wi[:, kv0:kr0], wi[:, MAIN_GSSD:dt0], wi[:, kr0:g0], jnp.zeros((d, LANE - MLA_ROPE_DIM), wi.dtype)],
            axis=1).astype(BF16)
        main = _matmul(hb, w_main, BF16, 1024, "in_proj_main")
        small = _matmul(hb, w_small, F32, SMALL_WIDTH, "in_proj_small")

        act = _conv_silu(main, _rows8(*[conv_w[l, i] for i in range(SSD_CONV_WIDTH)], conv_b[l]), batch, seq)
        yf, yb = _ssd_scan(act, small, _rows8(ssd_dt_bias[l].reshape(-1), ssd_a_log[l].reshape(-1)),
                           _rows8(jnp.repeat(ssd_d[l], SSD_HEAD_DIM)), batch, seq)
        y_ssd_n = _ssd_finish(yf, yb, main, _rows8(ssd_norm_w[l]))

        wk, wv_t = _kv_weights(w_ukv[l])
        q_nw_lanes = jnp.broadcast_to(mla_q_norm_w[l].astype(F32)[:, None], (MLA_Q_RANK, LANE))
        qt, k, vt = _mla_proj(small, q_nw_lanes, _q_weight_t(w_uq[l]), _rows8(mla_kv_norm_w[l]), wk, wv_t,
                              rope_tab, rope_tab_t, batch, seq)
        attn = _flash_attention(qt, k, vt)

        merged = _merge(y_ssd_n, w_ssd_out_b, attn, w_mla_out_b, main, l)
        hf, hb = _out_ln(merged, w_out_b, hf, _rows8(ln1_g[l], ln1_b[l]), l)
        ffn_act = _ffn_up(hb, w_gate_b, w_up_b, l)
        hf, hb = _ffn_down_ln(ffn_act, w_down_b, hf, _rows8(ln2_g[l], ln2_b[l]), l)
    return hf.reshape(batch, seq, d)
```

```python
import functools

import jax
import jax.numpy as jnp
from jax import lax
from jax.experimental import pallas as pl
from jax.experimental.pallas import tpu as pltpu

F32 = jnp.float32
BF16 = jnp.bfloat16

D_MODEL = 2048
DEPTH = 4
SSD_D_INNER = 4096
SSD_HEAD_DIM = 64
SSD_N_HEADS = 64
SSD_N_GROUPS = 8
SSD_HEADS_PER_GROUP = 8
SSD_D_STATE = 128
SSD_CONV_WIDTH = 5
SSD_CHUNK = 128
SSD_GROUP_WIDTH = SSD_HEADS_PER_GROUP * SSD_HEAD_DIM
SSD_BC_WIDTH = SSD_N_GROUPS * SSD_D_STATE
SSD_CONV_DIM = SSD_D_INNER + 2 * SSD_BC_WIDTH
MLA_N_HEADS = 16
MLA_Q_RANK = 512
MLA_KV_RANK = 512
MLA_NOPE_DIM = 128
MLA_ROPE_DIM = 64
MLA_V_DIM = 128
MLA_QK_DIM = MLA_NOPE_DIM + MLA_ROPE_DIM
MLA_QK_PAD = 256
ROPE_THETA = 10000.0
FLASH_TQ = 512
FLASH_TK = 256
FLASH_VT_ROWS = MLA_V_DIM + 16
LOG2_E = 1.4426950408889634
FFN_HIDDEN = 5632
DEEPNORM_ALPHA = (2 * DEPTH) ** 0.25
RMS_EPS = 1e-6
LN_EPS = 1e-5

MAIN_Z = 0
MAIN_XBC = SSD_D_INNER
MAIN_GSSD = MAIN_XBC + SSD_CONV_DIM
MAIN_GMLA = MAIN_GSSD + D_MODEL
MAIN_WIDTH = MAIN_GMLA + D_MODEL
SMALL_Q = 0
SMALL_CKV = MLA_Q_RANK
SMALL_DT = SMALL_CKV + MLA_KV_RANK
SMALL_KROPE = SMALL_DT + 2 * SSD_N_HEADS
SMALL_WIDTH = SMALL_KROPE + 128
ACT_B = SSD_D_INNER
ACT_C = SSD_D_INNER + SSD_BC_WIDTH

V7X_VMEM_BYTES = 64 * 1024 * 1024
VMEM_LIMIT_CAP = 56 * 1024 * 1024
VMEM_LIMIT_FLOOR = 32 * 1024 * 1024
LANE = 128
BF16_SUBLANE_TILE = 16


def _nbytes(shape, dtype):
    n = 1
    for s in shape:
        n *= s
    return n * jnp.dtype(dtype).itemsize


def _params(semantics, blocks, scratch_bytes=0, temp_bytes=0):
    need = 2 * sum(_nbytes(s, d) for s, d in blocks) + scratch_bytes + temp_bytes
    limit = min(max(need + need // 4, VMEM_LIMIT_FLOOR), VMEM_LIMIT_CAP)
    return pltpu.CompilerParams(dimension_semantics=semantics, vmem_limit_bytes=limit)


def _tile(n, pref):
    t = min(n, pref)
    assert n % t == 0, (n, pref)
    return t


def _dot(a, b):
    return jnp.dot(a, b, preferred_element_type=F32)


def _dot_nt(a, b):
    return lax.dot_general(a, b, (((1,), (1,)), ((), ())), preferred_element_type=F32)


def _mm_kernel(a_ref, w_ref, o_ref):
    o_ref[...] = _dot(a_ref[...].astype(BF16), w_ref[...]).astype(o_ref.dtype)


def _matmul(a, w, out_dtype, tn, name):
    m, k = a.shape
    n = w.shape[1]
    tm = _tile(m, 2048 if a.dtype == BF16 else 1024)
    tn = _tile(n, tn)
    blocks = [((tm, k), a.dtype), ((k, tn), w.dtype), ((tm, tn), out_dtype)]
    return pl.pallas_call(
        _mm_kernel,
        grid=(n // tn, m // tm),
        in_specs=[pl.BlockSpec((tm, k), lambda j, i: (i, 0)), pl.BlockSpec((k, tn), lambda j, i: (0, j))],
        out_specs=pl.BlockSpec((tm, tn), lambda j, i: (i, j)),
        out_shape=jax.ShapeDtypeStruct((m, n), out_dtype),
        compiler_params=_params(("arbitrary", "arbitrary"), blocks, temp_bytes=_nbytes((tm, tn), F32)),
        name=name,
    )(a, w)


def _silu(y):
    return 0.5 * y * (1.0 + jnp.tanh(0.5 * y))


def _conv_taps(xm2, xm1, x0, xp1, xp2, w):
    return xm2 * w[0:1] + xm1 * w[1:2] + x0 * w[2:3] + xp1 * w[3:4] + xp2 * w[4:5] + w[5:6]


def _conv_kernel(prev_ref, cur_ref, next_ref, w_ref, o_ref, *, n_seq_blocks):
    j = pl.program_id(1)
    cur = cur_ref[...].astype(F32)
    ts = cur.shape[0]
    has_prev = j > 0
    has_next = j < n_seq_blocks - 1
    e = BF16_SUBLANE_TILE
    p0 = jnp.where(has_prev, prev_ref[e - 2:e - 1, :].astype(F32), 0.0)
    p1 = jnp.where(has_prev, prev_ref[e - 1:e, :].astype(F32), 0.0)
    n0 = jnp.where(has_next, next_ref[0:1, :].astype(F32), 0.0)
    n1 = jnp.where(has_next, next_ref[1:2, :].astype(F32), 0.0)
    w = w_ref[...]
    y = _conv_taps(pltpu.roll(cur, 2, 0), pltpu.roll(cur, 1, 0), cur, pltpu.roll(cur, ts - 1, 0), pltpu.roll(cur, ts - 2, 0), w)
    o_ref[...] = _silu(y).astype(o_ref.dtype)
    row = lax.broadcasted_iota(jnp.int32, (e, cur.shape[1]), 0)
    top = cur[0:e]
    tm1 = jnp.where(row == 0, p1, pltpu.roll(top, 1, 0))
    tm2 = jnp.where(row == 0, p0, jnp.where(row == 1, p1, pltpu.roll(top, 2, 0)))
    tp1 = jnp.where(row == e - 1, cur[e:e + 1], pltpu.roll(top, e - 1, 0))
    tp2 = jnp.where(row == e - 2, cur[e:e + 1], jnp.where(row == e - 1, cur[e + 1:e + 2], pltpu.roll(top, e - 2, 0)))
    o_ref[0:e, :] = _silu(_conv_taps(tm2, tm1, top, tp1, tp2, w)).astype(o_ref.dtype)
    bot = cur[ts - e:ts]
    bm1 = jnp.where(row == 0, cur[ts - e - 1:ts - e], pltpu.roll(bot, 1, 0))
    bm2 = jnp.where(row == 0, cur[ts - e - 2:ts - e - 1], jnp.where(row == 1, cur[ts - e - 1:ts - e], pltpu.roll(bot, 2, 0)))
    bp1 = jnp.where(row == e - 1, n0, pltpu.roll(bot, e - 1, 0))
    bp2 = jnp.where(row == e - 2, n0, jnp.where(row == e - 1, n1, pltpu.roll(bot, e - 2, 0)))
    o_ref[ts - e:ts, :] = _silu(_conv_taps(bm2, bm1, bot, bp1, bp2, w)).astype(o_ref.dtype)


def _conv_silu(main, conv_prm, batch, seq):
    t = main.shape[0]
    ts = _tile(seq, 512)
    tc = 1024
    nsb = seq // ts
    col0 = MAIN_XBC // tc
    halo = BF16_SUBLANE_TILE
    rpb = ts // halo
    last_halo = t // halo - 1
    blocks = [((halo, tc), BF16), ((ts, tc), BF16), ((halo, tc), BF16), ((8, tc), F32), ((ts, tc), BF16)]
    return pl.pallas_call(
        functools.partial(_conv_kernel, n_seq_blocks=nsb),
        grid=(batch, nsb, SSD_CONV_DIM // tc),
        in_specs=[
            pl.BlockSpec((halo, tc), lambda b, j, c: (jnp.maximum((b * nsb + j) * rpb - 1, 0), col0 + c)),
            pl.BlockSpec((ts, tc), lambda b, j, c: (b * nsb + j, col0 + c)),
            pl.BlockSpec((halo, tc), lambda b, j, c: (jnp.minimum((b * nsb + j + 1) * rpb, last_halo), col0 + c)),
            pl.BlockSpec((8, tc), lambda b, j, c: (0, c)),
        ],
        out_specs=pl.BlockSpec((ts, tc), lambda b, j, c: (b * nsb + j, c)),
        out_shape=jax.ShapeDtypeStruct((t, SSD_CONV_DIM), BF16),
        compiler_params=_params(("arbitrary",) * 3, blocks, temp_bytes=8 * _nbytes((ts, tc), F32)),
        name="conv_silu",
    )(main, main, main, conv_prm)


def _ssd_kernel(xf_ref, bf_ref, cf_ref, dtf_ref, xb_ref, bb_ref, cb_ref, dtb_ref, prm_ref, dskip_ref,
                yf_ref, yb_ref, st_ref, col_ref, row_ref, *, groups_per_step):
    q = SSD_CHUNK
    k = pl.program_id(1)
    gi = pl.program_id(2)
    half = SSD_N_HEADS

    @pl.when(jnp.logical_and(k == 0, gi == 0))
    def _zero_state():
        st_ref[...] = jnp.zeros(st_ref.shape, st_ref.dtype)

    @pl.when(gi == 0)
    def _per_chunk_decay_terms():
        lane = lax.broadcasted_iota(jnp.int32, (q, 2 * half), 1)
        is_fwd = lane < half
        x = jnp.where(is_fwd, dtf_ref[...], dtb_ref[...]) + prm_ref[0:1, :]
        dt = jnp.maximum(x, 0.0) + jnp.log1p(jnp.exp(-jnp.abs(x)))
        a = -jnp.exp(prm_ref[1:2, :]) * dt
        r = lax.broadcasted_iota(jnp.int32, (q, q), 0)
        c = lax.broadcasted_iota(jnp.int32, (q, q), 1)
        tril = (r >= c).astype(BF16)
        triu = (r <= c).astype(BF16)
        a_hi = a.astype(BF16)
        a_lo = (a - a_hi.astype(F32)).astype(BF16)
        p = jnp.where(is_fwd, _dot(tril, a_hi) + _dot(tril, a_lo), _dot(triu, a_hi) + _dot(triu, a_lo))
        p_tot = jnp.where(is_fwd[0:1], p[q - 1:q, :], p[0:1, :])
        p_seg = p * LOG2_E
        col_ref[0] = p_seg
        col_ref[1] = jnp.exp(p)
        row_ref[0] = p_seg.T
        row_ref[1] = dt.T
        row_ref[2] = (dt * jnp.exp(p_tot - p)).T
        row_ref[3] = jnp.broadcast_to(jnp.exp(p_tot), (q, 2 * half)).T

    ri = lax.broadcasted_iota(jnp.int32, (q, q), 0)
    ci = lax.broadcasted_iota(jnp.int32, (q, q), 1)
    masks = (ri >= ci, ri <= ci)
    low_lanes = ci < SSD_HEAD_DIM
    dirs = ((xf_ref, bf_ref, cf_ref, yf_ref), (xb_ref, bb_ref, cb_ref, yb_ref))
    hd = SSD_HEAD_DIM
    zero_b = jnp.zeros((q, 2 * hd), BF16)
    er = lax.broadcasted_iota(jnp.int32, (2 * half, SSD_GROUP_WIDTH), 0)
    ec = lax.broadcasted_iota(jnp.int32, (2 * half, SSD_GROUP_WIDTH), 1)
    expand = tuple((er == d * half + ec // hd).astype(BF16) for d in range(2))
    for gg in range(groups_per_step):
        g = gi * groups_per_step + gg
        shift = lax.rem(2 * half - g * SSD_HEADS_PER_GROUP, 2 * half)
        p_cols = pltpu.roll(col_ref[0], shift, 1)
        ep_cols = pltpu.roll(col_ref[1], shift, 1)
        for d, (x_ref, b_ref, c_ref, y_ref) in enumerate(dirs):
            bg = b_ref[:, gg * SSD_D_STATE:(gg + 1) * SSD_D_STATE]
            cg = c_ref[:, gg * SSD_D_STATE:(gg + 1) * SSD_D_STATE]
            cb_masked = jnp.where(masks[d], _dot_nt(cg, bg), 0.0)
            bt = bg.astype(F32).T
            r0 = pl.multiple_of(d * half + g * SSD_HEADS_PER_GROUP, SSD_HEADS_PER_GROUP)
            p_rows = row_ref[0, pl.ds(r0, SSD_HEADS_PER_GROUP), :]
            dt_rows = row_ref[1, pl.ds(r0, SSD_HEADS_PER_GROUP), :]
            w_rows = row_ref[2, pl.ds(r0, SSD_HEADS_PER_GROUP), :]
            dec_rows = row_ref[3, pl.ds(r0, SSD_HEADS_PER_GROUP), :]
            ep_hi = ep_cols.astype(BF16)
            ep_lo = (ep_cols - ep_hi.astype(F32)).astype(BF16)
            inter = (_dot(ep_hi, expand[d]) + _dot(ep_lo, expand[d])) * _dot(cg, st_ref[d, g].astype(BF16))

            def head_terms(hh):
                li = d * half + hh
                seg = p_cols[:, li:li + 1] - p_rows[hh:hh + 1, :]
                l_mat = jnp.exp2(jnp.minimum(seg, 0.0)) * (cb_masked * dt_rows[hh:hh + 1, :])
                btw = bt * w_rows[hh:hh + 1, :]
                return l_mat.astype(BF16), btw.astype(BF16)

            for hp in range(SSD_HEADS_PER_GROUP // 2):
                h0, h1 = 2 * hp, 2 * hp + 1
                lo = (gg * SSD_HEADS_PER_GROUP + h0) * hd
                xp = x_ref[:, lo:lo + 2 * hd]
                x0 = jnp.where(low_lanes, xp, zero_b)
                x1 = jnp.where(low_lanes, zero_b, xp)
                sp = st_ref[d, g, :, h0 * hd:(h0 + 2) * hd]
                l0, w0 = head_terms(h0)
                l1, w1 = head_terms(h1)
                x_rhs = jnp.concatenate([x0, x1], axis=0)
                y_pair = _dot(jnp.concatenate([l0, l1], axis=1), x_rhs) + inter[:, h0 * hd:(h0 + 2) * hd]
                if d == 0:
                    y_pair = y_pair + dskip_ref[0:1, lo:lo + 2 * hd] * xp.astype(F32)
                y_ref[:, lo:lo + 2 * hd] = y_pair.astype(y_ref.dtype)
                dec = jnp.where(low_lanes[0:1], dec_rows[h0:h0 + 1, :], dec_rows[h1:h1 + 1, :])
                st_ref[d, g, :, h0 * hd:(h0 + 2) * hd] = sp * dec + _dot(jnp.concatenate([w0, w1], axis=1), x_rhs)


def _ssd_scan(act, small, ssd_prm, d_skip, batch, seq):
    t = act.shape[0]
    q = SSD_CHUNK
    nc = seq // q
    gps = SSD_N_GROUPS
    xw = gps * SSD_GROUP_WIDTH
    bw = gps * SSD_D_STATE
    b0 = ACT_B // bw
    c0 = ACT_C // bw
    dtc = SMALL_DT // LANE

    def fwd(col0):
        return lambda b, k, gi: (b * nc + k, col0 + gi)

    def bwd(col0):
        return lambda b, k, gi: (b * nc + nc - 1 - k, col0 + gi)

    blocks = 2 * [((q, xw), BF16), ((q, bw), BF16), ((q, bw), BF16), ((q, LANE), F32), ((q, xw), BF16)] + [((8, LANE), F32), ((8, xw), F32)]
    scratch = [
        pltpu.VMEM((2, SSD_N_GROUPS, SSD_D_STATE, SSD_GROUP_WIDTH), F32),
        pltpu.VMEM((2, q, LANE), F32),
        pltpu.VMEM((4, LANE, q), F32),
    ]
    scratch_bytes = _nbytes((2, SSD_N_GROUPS, SSD_D_STATE, SSD_GROUP_WIDTH), F32) + 6 * _nbytes((q, LANE), F32)
    return pl.pallas_call(
        functools.partial(_ssd_kernel, groups_per_step=gps),
        grid=(batch, nc, SSD_N_GROUPS // gps),
        in_specs=[
            pl.BlockSpec((q, xw), fwd(0)), pl.BlockSpec((q, bw), fwd(b0)), pl.BlockSpec((q, bw), fwd(c0)),
            pl.BlockSpec((q, LANE), lambda b, k, gi: (b * nc + k, dtc)),
            pl.BlockSpec((q, xw), bwd(0)), pl.BlockSpec((q, bw), bwd(b0)), pl.BlockSpec((q, bw), bwd(c0)),
            pl.BlockSpec((q, LANE), lambda b, k, gi: (b * nc + nc - 1 - k, dtc)),
            pl.BlockSpec((8, LANE), lambda b, k, gi: (0, 0)),
            pl.BlockSpec((8, xw), lambda b, k, gi: (0, gi)),
        ],
        out_specs=[pl.BlockSpec((q, xw), fwd(0)), pl.BlockSpec((q, xw), bwd(0))],
        out_shape=[jax.ShapeDtypeStruct((t, SSD_D_INNER), BF16)] * 2,
        scratch_shapes=scratch,
        compiler_params=_params(("arbitrary",) * 3, blocks, scratch_bytes, temp_bytes=64 * _nbytes((q, q), F32)),
        name="ssd_scan",
    )(act, act, act, small, act, act, act, small, ssd_prm, d_skip)


def _ssd_finish_kernel(yf_ref, yb_ref, z_ref, prm_ref, o_ref):
    z = z_ref[...].astype(F32)
    y = yf_ref[...].astype(F32) + yb_ref[...].astype(F32)
    y = y * (z * jax.nn.sigmoid(z))
    gw = SSD_GROUP_WIDTH
    for g in range(SSD_N_GROUPS):
        yg = y[:, g * gw:(g + 1) * gw]
        ms = jnp.mean(yg * yg, axis=-1, keepdims=True)
        o_ref[:, g * gw:(g + 1) * gw] = (yg * lax.rsqrt(ms + RMS_EPS) * prm_ref[0:1, g * gw:(g + 1) * gw]).astype(o_ref.dtype)


def _ssd_finish(yf, yb, main, fin_prm):
    t = yf.shape[0]
    ts = _tile(t, 256)
    di = SSD_D_INNER
    row_spec = pl.BlockSpec((ts, di), lambda i: (i, 0))
    blocks = 4 * [((ts, di), BF16)] + [((8, di), F32)]
    return pl.pallas_call(
        _ssd_finish_kernel,
        grid=(t // ts,),
        in_specs=[row_spec, row_spec, row_spec, pl.BlockSpec((8, di), lambda i: (0, 0))],
        out_specs=row_spec,
        out_shape=jax.ShapeDtypeStruct((t, di), BF16),
        compiler_params=_params(("arbitrary",), blocks, temp_bytes=4 * _nbytes((ts, di), F32)),
        name="ssd_finish",
    )(yf, yb, main, fin_prm)


def _rope_lanes(t, tab):
    return (t * tab[:, 0:LANE] + pltpu.roll(t, LANE - MLA_ROPE_DIM // 2, 1) * tab[:, LANE:2 * LANE]
            + pltpu.roll(t, MLA_ROPE_DIM // 2, 1) * tab[:, 2 * LANE:3 * LANE])


def _rms_rows(x, w):
    return x * lax.rsqrt(jnp.mean(x * x, axis=-1, keepdims=True) + RMS_EPS) * w


def _qproj_kernel(x_ref, nw_ref, w_ref, tab_ref, qt_ref, *, scale):
    xt = x_ref[...].T
    tm = xt.shape[1]
    inv = lax.rsqrt(jnp.mean(xt * xt, axis=0, keepdims=True) + RMS_EPS)
    nw = jnp.concatenate([nw_ref[...]] * (tm // LANE), axis=1)
    xnt = (xt * inv * nw).astype(BF16)
    half = MLA_ROPE_DIM // 2
    cos = tab_ref[0:half, :]
    sin = tab_ref[half:2 * half, :]
    r0 = MLA_NOPE_DIM
    for h in range(MLA_N_HEADS):
        qt = _dot(w_ref[h * MLA_QK_PAD:(h + 1) * MLA_QK_PAD, :], xnt)
        x1 = qt[r0:r0 + half]
        x2 = qt[r0 + half:r0 + 2 * half]
        qt_ref[0, h, 0:r0, :] = (qt[0:r0] * scale).astype(qt_ref.dtype)
        qt_ref[0, h, r0:r0 + half, :] = ((x1 * cos - x2 * sin) * scale).astype(qt_ref.dtype)
        qt_ref[0, h, r0 + half:r0 + 2 * half, :] = ((x2 * cos + x1 * sin) * scale).astype(qt_ref.dtype)
        qt_ref[0, h, r0 + 2 * half:MLA_QK_PAD, :] = jnp.zeros((MLA_QK_PAD - MLA_QK_DIM, tm), qt_ref.dtype)


def _kvproj_kernel(x_ref, kr_ref, nw_ref, wk_ref, wvt_ref, tab_ref, k_ref, vt_ref, *, tk):
    cn = _rms_rows(x_ref[...], nw_ref[0:1, :])
    tm = cn.shape[0]
    k_all = _dot(cn.astype(BF16), wk_ref[...])
    vt_all = _dot(wvt_ref[...], cn.T.astype(BF16))
    k_rope = _rope_lanes(kr_ref[...], tab_ref[...]).astype(k_ref.dtype)
    pad_rows = FLASH_VT_ROWS - MLA_V_DIM
    ones_rows = (lax.broadcasted_iota(jnp.int32, (pad_rows, tk), 0) == 0).astype(vt_ref.dtype)
    for h in range(MLA_N_HEADS):
        k_ref[0, h, :, 0:LANE] = k_all[:, h * MLA_NOPE_DIM:(h + 1) * MLA_NOPE_DIM].astype(k_ref.dtype)
        k_ref[0, h, :, LANE:2 * LANE] = k_rope
        for c in range(tm // tk):
            vt_ref[0, h, c, 0:MLA_V_DIM, :] = vt_all[h * MLA_V_DIM:(h + 1) * MLA_V_DIM, c * tk:(c + 1) * tk].astype(vt_ref.dtype)
            vt_ref[0, h, c, MLA_V_DIM:FLASH_VT_ROWS, :] = ones_rows


def _mla_proj(small, q_nw_lanes, wq_t, kv_nw, wk, wv_t, rope_tab, rope_tab_t, batch, seq):
    t = small.shape[0]
    tk = _tile(seq, FLASH_TK)
    tm = _tile(seq, max(512, tk))
    nsb = seq // tm
    nh = MLA_N_HEADS
    qw = nh * MLA_QK_PAD
    kw = nh * MLA_NOPE_DIM
    vw = nh * MLA_V_DIM
    q_blocks = [((tm, MLA_Q_RANK), F32), ((MLA_Q_RANK, LANE), F32), ((qw, MLA_Q_RANK), BF16), ((MLA_ROPE_DIM, tm), F32),
                ((nh, MLA_QK_PAD, tm), BF16)]
    qt = pl.pallas_call(
        functools.partial(_qproj_kernel, scale=MLA_QK_DIM ** -0.5 * LOG2_E),
        grid=(t // tm,),
        in_specs=[pl.BlockSpec((tm, MLA_Q_RANK), lambda i: (i, SMALL_Q // MLA_Q_RANK)),
                  pl.BlockSpec((MLA_Q_RANK, LANE), lambda i: (0, 0)),
                  pl.BlockSpec((qw, MLA_Q_RANK), lambda i: (0, 0)),
                  pl.BlockSpec((MLA_ROPE_DIM, tm), lambda i: (0, i % nsb))],
        out_specs=pl.BlockSpec((1, nh, MLA_QK_PAD, tm), lambda i: (i // nsb, 0, 0, i % nsb)),
        out_shape=jax.ShapeDtypeStruct((batch, nh, MLA_QK_PAD, seq), BF16),
        compiler_params=_params(("arbitrary",), q_blocks, temp_bytes=6 * _nbytes((tm, MLA_Q_RANK), F32)),
        name="mla_q_proj",
    )(small, q_nw_lanes, wq_t, rope_tab_t)
    kv_blocks = [((tm, MLA_KV_RANK), F32), ((tm, LANE), F32), ((8, MLA_KV_RANK), F32), ((MLA_KV_RANK, kw), BF16),
                 ((vw, MLA_KV_RANK), BF16), ((tm, 3 * LANE), F32), ((nh, tm, MLA_QK_PAD), BF16), ((nh, tm // tk, FLASH_VT_ROWS, tk), BF16)]
    k, vt = pl.pallas_call(
        functools.partial(_kvproj_kernel, tk=tk),
        grid=(t // tm,),
        in_specs=[pl.BlockSpec((tm, MLA_KV_RANK), lambda i: (i, SMALL_CKV // MLA_KV_RANK)),
                  pl.BlockSpec((tm, LANE), lambda i: (i, SMALL_KROPE // LANE)),
                  pl.BlockSpec((8, MLA_KV_RANK), lambda i: (0, 0)),
                  pl.BlockSpec((MLA_KV_RANK, kw), lambda i: (0, 0)),
                  pl.BlockSpec((vw, MLA_KV_RANK), lambda i: (0, 0)),
                  pl.BlockSpec((tm, 3 * LANE), lambda i: (i % nsb, 0))],
        out_specs=[pl.BlockSpec((1, nh, tm, MLA_QK_PAD), lambda i: (i // nsb, 0, i % nsb, 0)),
                   pl.BlockSpec((1, nh, tm // tk, FLASH_VT_ROWS, tk), lambda i: (i // nsb, 0, i % nsb, 0, 0))],
        out_shape=[jax.ShapeDtypeStruct((batch, nh, seq, MLA_QK_PAD), BF16),
                   jax.ShapeDtypeStruct((batch, nh, seq // tk, FLASH_VT_ROWS, tk), BF16)],
        compiler_params=_params(("arbitrary",), kv_blocks, temp_bytes=2 * _nbytes((tm, kw), F32) + 4 * _nbytes((tm, MLA_KV_RANK), F32)),
        name="mla_kv_proj",
    )(small, small, kv_nw, wk, wv_t, rope_tab)
    return qt, k, vt


def _flash_kernel(qt_ref, k_ref, vt_ref, o_ref, s0_ref, s1_ref, p0_ref, p1_ref, acc_ref):
    tq = qt_ref.shape[3]
    n_kv, _, tk = vt_ref.shape[2:]

    def qk(j, s_ref):
        s = _dot(k_ref[0, 0, j * tk:(j + 1) * tk, :], qt_ref[0, 0])
        s_ref[...] = s
        return jnp.max(s, axis=0, keepdims=True)

    def soft(m, cm, s_ref, p_ref):
        m_new = jnp.maximum(m, cm)
        p_ref[...] = jnp.exp2(s_ref[...] - m_new).astype(p_ref.dtype)
        return m_new, jnp.exp2(m - m_new)

    def pv(j, alpha, p_ref):
        acc_ref[...] = alpha * acc_ref[...] + _dot(vt_ref[0, 0, j], p_ref[...])

    def step(j, m, alpha_prev, cm, s_cur, s_nxt, p_prev, p_cur):
        cm_next = qk(j + 1, s_nxt)
        pv(j - 1, alpha_prev, p_prev)
        m, alpha = soft(m, cm, s_cur, p_cur)
        return m, alpha, cm_next

    acc_ref[...] = jnp.zeros(acc_ref.shape, acc_ref.dtype)
    m = jnp.full((1, tq), -jnp.inf, F32)
    m, alpha = soft(m, qk(0, s0_ref), s0_ref, p0_ref)
    cm = qk(1, s1_ref)
    for j in range(1, n_kv - 1, 2):
        m, alpha, cm = step(j, m, alpha, cm, s1_ref, s0_ref, p0_ref, p1_ref)
        m, alpha, cm = step(j + 1, m, alpha, cm, s0_ref, s1_ref, p1_ref, p0_ref)
    pv(n_kv - 2, alpha, p0_ref)
    m, alpha = soft(m, cm, s1_ref, p1_ref)
    pv(n_kv - 1, alpha, p1_ref)
    o_ref[...] = (acc_ref[0:MLA_V_DIM, :] / acc_ref[MLA_V_DIM:MLA_V_DIM + 1, :]).T.astype(o_ref.dtype)


def _flash_attention(qt, k, vt):
    batch, nh, seq, _ = k.shape
    n_kv, vt_rows, tk = vt.shape[2:]
    assert n_kv >= 2 and n_kv % 2 == 0, n_kv
    tq = _tile(seq, FLASH_TQ)
    nq = seq // tq
    blocks = [((MLA_QK_PAD, tq), BF16), ((seq, MLA_QK_PAD), BF16), ((n_kv, vt_rows, tk), BF16), ((tq, MLA_V_DIM), BF16)]
    scratch = [pltpu.VMEM((tk, tq), F32), pltpu.VMEM((tk, tq), F32), pltpu.VMEM((tk, tq), BF16), pltpu.VMEM((tk, tq), BF16),
               pltpu.VMEM((vt_rows, tq), F32)]
    scratch_bytes = 3 * _nbytes((tk, tq), F32) + _nbytes((vt_rows, tq), F32)
    return pl.pallas_call(
        _flash_kernel,
        grid=(batch, nh, nq),
        in_specs=[
            pl.BlockSpec((1, 1, MLA_QK_PAD, tq), lambda b, h, i: (b, h, 0, i)),
            pl.BlockSpec((1, 1, seq, MLA_QK_PAD), lambda b, h, i: (b, h, 0, 0)),
            pl.BlockSpec((1, 1, n_kv, vt_rows, tk), lambda b, h, i: (b, h, 0, 0, 0)),
        ],
        out_specs=pl.BlockSpec((tq, MLA_V_DIM), lambda b, h, i: (b * nq + i, h)),
        out_shape=jax.ShapeDtypeStruct((batch * seq, nh * MLA_V_DIM), BF16),
        scratch_shapes=scratch,
        compiler_params=_params(("arbitrary",) * 3, blocks, scratch_bytes, temp_bytes=4 * _nbytes((tk, tq), F32)),
        name="mla_flash_attention",
    )(qt, k, vt)


def _merge_kernel(a1_ref, w1_ref, a2_ref, w2_ref, g1_ref, g2_ref, o_ref):
    y1 = _dot(a1_ref[...], w1_ref[...])
    y2 = _dot(a2_ref[...], w2_ref[...])
    g1 = jax.nn.sigmoid(g1_ref[...].astype(F32))
    g2 = jax.nn.sigmoid(g2_ref[...].astype(F32))
    o_ref[...] = (g1 * y1 + g2 * y2).astype(o_ref.dtype)


def _merge(y_ssd_n, w_ssd_out, attn, w_mla_out, main, layer):
    t = y_ssd_n.shape[0]
    tm = _tile(t, 1024)
    tn = 512
    k1, k2 = SSD_D_INNER, MLA_N_HEADS * MLA_V_DIM
    streamed = [((tm, k1), BF16), ((tm, k2), BF16), ((tm, tn), BF16), ((tm, tn), BF16), ((tm, tn), BF16)]
    resident = _nbytes((k1, tn), BF16) + _nbytes((k2, tn), BF16)
    once = pl.Buffered(1)
    return pl.pallas_call(
        _merge_kernel,
        grid=(D_MODEL // tn, t // tm),
        in_specs=[
            pl.BlockSpec((tm, k1), lambda j, i: (i, 0)), pl.BlockSpec((None, k1, tn), lambda j, i: (layer, 0, j), pipeline_mode=once),
            pl.BlockSpec((tm, k2), lambda j, i: (i, 0)), pl.BlockSpec((None, k2, tn), lambda j, i: (layer, 0, j), pipeline_mode=once),
            pl.BlockSpec((tm, tn), lambda j, i: (i, MAIN_GSSD // tn + j)),
            pl.BlockSpec((tm, tn), lambda j, i: (i, MAIN_GMLA // tn + j)),
        ],
        out_specs=pl.BlockSpec((tm, tn), lambda j, i: (i, j)),
        out_shape=jax.ShapeDtypeStruct((t, D_MODEL), BF16),
        compiler_params=_params(("arbitrary",) * 2, streamed, scratch_bytes=resident, temp_bytes=4 * _nbytes((tm, tn), F32)),
        name="branch_merge",
    )(y_ssd_n, w_ssd_out, attn, w_mla_out, main, main)


def _layer_norm_rows(r, prm_ref):
    mu = jnp.mean(r, axis=-1, keepdims=True)
    c = r - mu
    var = jnp.mean(c * c, axis=-1, keepdims=True)
    return c * lax.rsqrt(var + LN_EPS) * prm_ref[0:1, :] + prm_ref[1:2, :]


def _out_ln_kernel(a_ref, w_ref, h_ref, prm_ref, of_ref, ob_ref):
    y = _layer_norm_rows(DEEPNORM_ALPHA * h_ref[...] + _dot(a_ref[...], w_ref[...]), prm_ref)
    of_ref[...] = y
    ob_ref[...] = y.astype(ob_ref.dtype)


def _out_ln(merged, w_out, h, ln_prm, layer):
    t = h.shape[0]
    tm = _tile(t, 512)
    d = D_MODEL
    streamed = [((tm, d), BF16), ((tm, d), F32), ((8, d), F32), ((tm, d), F32), ((tm, d), BF16)]
    row = lambda i: (i, 0)
    return pl.pallas_call(
        _out_ln_kernel,
        grid=(t // tm,),
        in_specs=[pl.BlockSpec((tm, d), row), pl.BlockSpec((None, d, d), lambda i: (layer, 0, 0), pipeline_mode=pl.Buffered(1)), pl.BlockSpec((tm, d), row),
                  pl.BlockSpec((8, d), lambda i: (0, 0))],
        out_specs=[pl.BlockSpec((tm, d), row), pl.BlockSpec((tm, d), row)],
        out_shape=[jax.ShapeDtypeStruct((t, d), F32), jax.ShapeDtypeStruct((t, d), BF16)],
        compiler_params=_params(("arbitrary",), streamed, scratch_bytes=_nbytes((d, d), BF16), temp_bytes=3 * _nbytes((tm, d), F32)),
        name="mixer_out_ln",
    )(merged, w_out, h, ln_prm)


def _ffn_up_kernel(a_ref, wg_ref, wu_ref, o_ref):
    a = a_ref[...]
    gate = _dot(a, wg_ref[...])
    o_ref[...] = (gate * jax.nn.sigmoid(gate) * _dot(a, wu_ref[...])).astype(o_ref.dtype)


def _ffn_up(hb, wg, wu, layer):
    t = hb.shape[0]
    tm = _tile(t, 1024)
    tn = 512
    d = D_MODEL
    blocks = [((tm, d), BF16), ((d, tn), BF16), ((d, tn), BF16), ((tm, tn), BF16)]
    return pl.pallas_call(
        _ffn_up_kernel,
        grid=(FFN_HIDDEN // tn, t // tm),
        in_specs=[pl.BlockSpec((tm, d), lambda j, i: (i, 0)), pl.BlockSpec((None, d, tn), lambda j, i: (layer, 0, j)),
                  pl.BlockSpec((None, d, tn), lambda j, i: (layer, 0, j))],
        out_specs=pl.BlockSpec((tm, tn), lambda j, i: (i, j)),
        out_shape=jax.ShapeDtypeStruct((t, FFN_HIDDEN), BF16),
        compiler_params=_params(("arbitrary",) * 2, blocks, temp_bytes=3 * _nbytes((tm, tn), F32)),
        name="ffn_up",
    )(hb, wg, wu)


def _ffn_down_ln_kernel(a_ref, w_ref, h_ref, prm_ref, of_ref, ob_ref):
    y = _layer_norm_rows(DEEPNORM_ALPHA * h_ref[...] + _dot(a_ref[...], w_ref[...]), prm_ref)
    of_ref[...] = y
    ob_ref[...] = y.astype(ob_ref.dtype)


def _ffn_down_ln(act, w_down, h, ln_prm, layer):
    t = h.shape[0]
    tm = _tile(t, 256)
    d = D_MODEL
    kk = FFN_HIDDEN
    row = lambda i: (i, 0)
    streamed = [((tm, kk), BF16), ((tm, d), F32), ((8, d), F32), ((tm, d), F32), ((tm, d), BF16)]
    return pl.pallas_call(
        _ffn_down_ln_kernel,
        grid=(t // tm,),
        in_specs=[pl.BlockSpec((tm, kk), row), pl.BlockSpec((None, kk, d), lambda i: (layer, 0, 0), pipeline_mode=pl.Buffered(1)),
                  pl.BlockSpec((tm, d), row), pl.BlockSpec((8, d), lambda i: (0, 0))],
        out_specs=[pl.BlockSpec((tm, d), row), pl.BlockSpec((tm, d), row)],
        out_shape=[jax.ShapeDtypeStruct((t, d), F32), jax.ShapeDtypeStruct((t, d), BF16)],
        compiler_params=_params(("arbitrary",), streamed, scratch_bytes=_nbytes((kk, d), BF16), temp_bytes=3 * _nbytes((tm, d), F32)),
        name="ffn_down_ln",
    )(act, w_down, h, ln_prm)


def _rows8(*rows):
    n = rows[0].shape[0]
    pad = jnp.zeros((8 - len(rows), n), F32)
    return jnp.concatenate([jnp.stack([r.astype(F32) for r in rows]), pad], axis=0)


def _rope_table(seq):
    pos = jnp.arange(seq, dtype=F32)
    inv_freq = ROPE_THETA ** (-jnp.arange(0, MLA_ROPE_DIM, 2, dtype=F32) / MLA_ROPE_DIM)
    ang = pos[:, None] * inv_freq[None, :]
    cos, sin = jnp.cos(ang), jnp.sin(ang)
    half = MLA_ROPE_DIM // 2
    z = lambda n: jnp.zeros((seq, n), F32)
    lanes = jnp.concatenate([cos, cos, z(LANE - 2 * half), -sin, z(LANE - half), z(half), sin, z(LANE - 2 * half)], axis=1)
    return lanes, jnp.concatenate([cos.T, sin.T], axis=0)


def _q_weight_t(w_uq):
    r = w_uq.shape[0]
    w = w_uq.reshape(r, MLA_N_HEADS, MLA_QK_DIM)
    w = jnp.pad(w, ((0, 0), (0, 0), (0, MLA_QK_PAD - MLA_QK_DIM)))
    return w.reshape(r, MLA_N_HEADS * MLA_QK_PAD).T.astype(BF16)


def _kv_weights(w_ukv):
    r = w_ukv.shape[0]
    w = w_ukv.reshape(r, MLA_N_HEADS, MLA_NOPE_DIM + MLA_V_DIM)
    wk = w[:, :, :MLA_NOPE_DIM].reshape(r, MLA_N_HEADS * MLA_NOPE_DIM)
    wv = w[:, :, MLA_NOPE_DIM:].reshape(r, MLA_N_HEADS * MLA_V_DIM)
    return wk.astype(BF16), wv.T.astype(BF16)


def kernel(x, w_in, conv_w, conv_b, ssd_a_log, ssd_dt_bias, ssd_d, ssd_norm_w, w_ssd_out, mla_q_norm_w, w_uq,
           mla_kv_norm_w, w_ukv, w_mla_out, w_out, ln1_g, ln1_b, w_ffn_gate, w_ffn_up, w_ffn_down, ln2_g, ln2_b):
    batch, seq, d = x.shape
    assert d == D_MODEL and seq % SSD_CHUNK == 0
    t = batch * seq
    hf = x.reshape(t, d)
    hb = hf
    rope_tab, rope_tab_t = _rope_table(seq)
    dt0 = MAIN_GSSD + 2 * SSD_N_HEADS
    q0 = dt0
    kv0 = q0 + MLA_Q_RANK
    kr0 = kv0 + MLA_KV_RANK
    g0 = kr0 + MLA_ROPE_DIM
    w_ssd_out_b, w_mla_out_b, w_out_b = w_ssd_out.astype(BF16), w_mla_out.astype(BF16), w_out.astype(BF16)
    w_gate_b, w_up_b, w_down_b = w_ffn_gate.astype(BF16), w_ffn_up.astype(BF16), w_ffn_down.astype(BF16)
    for l in range(w_in.shape[0]):
        wi = w_in[l]
        w_main = jnp.concatenate([wi[:, :MAIN_GSSD], wi[:, g0:]], axis=1).astype(BF16)
        w_small = jnp.concatenate(
            [wi[:, q0:kv0], wi[:, kv0:kr0], wi[:, MAIN_GSSD:dt0], wi[:, kr0:g0], jnp.zeros((d, LANE - MLA_ROPE_DIM), wi.dtype)],
            axis=1).astype(BF16)
        main = _matmul(hb, w_main, BF16, 1024, "in_proj_main")
        small = _matmul(hb, w_small, F32, SMALL_WIDTH, "in_proj_small")

        act = _conv_silu(main, _rows8(*[conv_w[l, i] for i in range(SSD_CONV_WIDTH)], conv_b[l]), batch, seq)
        yf, yb = _ssd_scan(act, small, _rows8(ssd_dt_bias[l].reshape(-1), ssd_a_log[l].reshape(-1)),
                           _rows8(jnp.repeat(ssd_d[l], SSD_HEAD_DIM)), batch, seq)
        y_ssd_n = _ssd_finish(yf, yb, main, _rows8(ssd_norm_w[l]))

        wk, wv_t = _kv_weights(w_ukv[l])
        q_nw_lanes = jnp.broadcast_to(mla_q_norm_w[l].astype(F32)[:, None], (MLA_Q_RANK, LANE))
        qt, k, vt = _mla_proj(small, q_nw_lanes, _q_weight_t(w_uq[l]), _rows8(mla_kv_norm_w[l]), wk, wv_t,
                              rope_tab, rope_tab_t, batch, seq)
        attn = _flash_attention(qt, k, vt)

        merged = _merge(y_ssd_n, w_ssd_out_b, attn, w_mla_out_b, main, l)
        hf, hb = _out_ln(merged, w_out_b, hf, _rows8(ln1_g[l], ln1_b[l]), l)
        ffn_act = _ffn_up(hb, w_gate_b, w_up_b, l)
        hf, hb = _ffn_down_ln(ffn_act, w_down_b, hf, _rows8(ln2_g[l], ln2_b[l]), l)
    return hf.reshape(batch, seq, d)
```
